```python
import math
import jax, jax.numpy as jnp
from jax import lax
import numpy as np

D_MODEL = 1024
BATCH = 8
SEQ = 4096
DEPTH = 4

D_CONV = D_MODEL // 2
CONV_WIDTH = 31
MLSTM_HEADS = 4
D_MLSTM = D_MODEL // 2
MLSTM_HEAD_DIM = D_MLSTM // MLSTM_HEADS
MLSTM_CHUNK = 128
D_GMLP = D_MODEL // 2
GMLP_GROUPS = 4
GMLP_GROUP_DIM = D_GMLP // GMLP_GROUPS
GMLP_CHUNK = 128
SB_HEADS = 8
D_SB = D_MODEL // 2
SB_HEAD_DIM = D_SB // SB_HEADS
SB_QBLOCK = 128
D_FF = 2816
N_EXPERTS = 8
TOP_K = 2
D_MIX_EVEN = D_CONV + D_MLSTM
D_MIX_ODD = D_GMLP + D_SB
IN_EVEN = 2 * D_CONV + 4 * D_MLSTM + 2 * MLSTM_HEADS
IN_ODD = 2 * D_GMLP + 3 * D_SB
N_EVEN = (DEPTH + 1) // 2
N_ODD = DEPTH // 2
ALPHA = (2 * DEPTH) ** 0.25
BETA_INIT = (8 * DEPTH) ** -0.25
LN_EPS = 1e-5

kernel_name = "hybrid_conv_mlstm_gmlp_stickbreak_moe_deepnorm"


def layer_norm(x, g, b):
    x32 = x.astype(jnp.float32)
    mu = jnp.mean(x32, axis=-1, keepdims=True)
    var = jnp.mean(jnp.square(x32 - mu), axis=-1, keepdims=True)
    return ((x32 - mu) * lax.rsqrt(var + LN_EPS) * g + b).astype(x.dtype)


def swiglu(h, w1, w3, w2):
    return (jax.nn.silu(h @ w1) * (h @ w3)) @ w2


def causal_depthwise_conv(x, w, b):
    y = lax.conv_general_dilated(
        x, w[:, None, :].astype(x.dtype), window_strides=(1,),
        padding=[(CONV_WIDTH - 1, 0)], dimension_numbers=("NWC", "WIO", "NWC"),
        feature_group_count=x.shape[-1])
    return y + b.astype(x.dtype)


def mlstm_chunkwise(q, k, v, i_pre, f_pre):
    B, S, H, d = q.shape
    L = MLSTM_CHUNK
    nc = S // L
    f32 = jnp.float32

    def chunks(t):
        return t.astype(f32).reshape(B, nc, L, H, -1).transpose(0, 3, 1, 2, 4)

    qc, kc, vc = chunks(q), chunks(k) * (d ** -0.5), chunks(v)
    logf = chunks(jax.nn.log_sigmoid(f_pre.astype(f32))[..., None])[..., 0]
    ig = chunks(i_pre[..., None])[..., 0]
    b = jnp.cumsum(logf, axis=-1)
    g = b[..., -1]
    causal = jnp.tril(jnp.ones((L, L), bool))
    log_d = jnp.where(causal, b[..., :, None] - b[..., None, :] + ig[..., None, :], -jnp.inf)
    log_w = g[..., None] - b + ig
    a = jnp.max(log_w, axis=-1)
    w = jnp.exp(log_w - a[..., None])
    kv_loc = jnp.einsum("bhcld,bhcle->bhcde", kc * w[..., None], vc)
    n_loc = jnp.einsum("bhcl,bhcld->bhcd", w, kc)

    def step(carry, inp):
        C, n, m = carry
        kv_c, n_c, g_c, a_c = inp
        m_new = jnp.maximum(g_c + m, a_c)
        s_old = jnp.exp(g_c + m - m_new)
        s_new = jnp.exp(a_c - m_new)
        C_new = s_old[..., None, None] * C + s_new[..., None, None] * kv_c
        n_new = s_old[..., None] * n + s_new[..., None] * n_c
        return (C_new, n_new, m_new), (C, n, m)

    init = (jnp.zeros((B, H, d, d), f32), jnp.zeros((B, H, d), f32), jnp.zeros((B, H), f32))
    xs = (jnp.moveaxis(kv_loc, 2, 0), jnp.moveaxis(n_loc, 2, 0), jnp.moveaxis(g, 2, 0), jnp.moveaxis(a, 2, 0))
    _, (C_prev, n_prev, m_prev) = lax.scan(step, init, xs)
    C_prev = jnp.moveaxis(C_prev, 0, 2)
    n_prev = jnp.moveaxis(n_prev, 0, 2)
    m_prev = jnp.moveaxis(m_prev, 0, 2)

    log_inter = b + m_prev[..., None]
    m_t = jnp.maximum(log_inter, jnp.max(log_d, axis=-1))
    d_intra = jnp.exp(log_d - m_t[..., None])
    s_inter = jnp.exp(log_inter - m_t)
    qk = jnp.einsum("bhcld,bhcsd->bhcls", qc, kc) * d_intra
    num = jnp.einsum("bhcls,bhcse->bhcle", qk, vc) + s_inter[..., None] * jnp.einsum("bhcld,bhcde->bhcle", qc, C_prev)
    den = jnp.sum(qk, axis=-1) + s_inter * jnp.einsum("bhcld,bhcd->bhcl", qc, n_prev)
    h = num / jnp.maximum(jnp.abs(den), jnp.exp(-m_t))[..., None]
    return h.transpose(0, 2, 3, 1, 4).reshape(B, S, H, d)


def stick_breaking_attention(q, k, v):
    B, S, H, d = q.shape
    scale = d ** -0.5
    outs = []
    for blk in range(S // SB_QBLOCK):
        q0 = blk * SB_QBLOCK
        kend = q0 + SB_QBLOCK
        z = jnp.einsum("blhd,bshd->bhls", q[:, q0:kend], k[:, :kend]).astype(jnp.float32) * scale
        t_idx = q0 + jnp.arange(SB_QBLOCK)[:, None]
        s_idx = jnp.arange(kend)[None, :]
        mask = s_idx < t_idx
        log_1mb = jnp.where(mask, jax.nn.log_sigmoid(-z), 0.0)
        rc = lax.cumsum(log_1mb, axis=3, reverse=True) - log_1mb
        att = jnp.where(mask, jnp.exp(jax.nn.log_sigmoid(z) + rc), 0.0)
        outs.append(jnp.einsum("bhls,bshd->blhd", att, v[:, :kend].astype(jnp.float32)))
    return jnp.concatenate(outs, axis=1)


def conv_mlstm_mixer(h, w_in, gate_bias, conv_w, conv_b, norm_g, norm_b, head_g, w_out):
    B, S, _ = h.shape
    p = h @ w_in
    cuts = [D_CONV, 2 * D_CONV, 2 * D_CONV + D_MLSTM, 2 * D_CONV + 2 * D_MLSTM,
            2 * D_CONV + 3 * D_MLSTM, 2 * D_CONV + 4 * D_MLSTM]
    a_val, a_gate, q, k, v, o, gates = jnp.split(p, cuts, axis=-1)
    a = causal_depthwise_conv(a_val * jax.nn.sigmoid(a_gate), conv_w, conv_b)
    a = jax.nn.silu(layer_norm(a, norm_g, norm_b))
    gates = gates.astype(jnp.float32) + gate_bias
    i_pre, f_pre = gates[..., :MLSTM_HEADS], gates[..., MLSTM_HEADS:]
    shp = (B, S, MLSTM_HEADS, MLSTM_HEAD_DIM)
    hb = mlstm_chunkwise(q.reshape(shp), k.reshape(shp), v.reshape(shp), i_pre, f_pre)
    mu = jnp.mean(hb, axis=-1, keepdims=True)
    var = jnp.mean(jnp.square(hb - mu), axis=-1, keepdims=True)
    hb = ((hb - mu) * lax.rsqrt(var + LN_EPS)).reshape(B, S, D_MLSTM) * head_g
    hb = (jax.nn.sigmoid(o.astype(jnp.float32)) * hb).astype(h.dtype)
    return jnp.concatenate([a, hb], axis=-1) @ w_out


def gmlp_sb_mixer(h, w_in, v_g, v_b, w_s, b_s, w_out):
    B, S, _ = h.shape
    p = h @ w_in
    uv, q, k, v = jnp.split(p, [2 * D_GMLP, 2 * D_GMLP + D_SB, 2 * D_GMLP + 2 * D_SB], axis=-1)
    u, z = jnp.split(jax.nn.gelu(uv, approximate=False), 2, axis=-1)
    z = layer_norm(z, v_g, v_b)
    nc = S // GMLP_CHUNK
    z = z.reshape(B, nc, GMLP_CHUNK, GMLP_GROUPS, GMLP_GROUP_DIM)
    w_causal = jnp.where(jnp.tril(jnp.ones((GMLP_CHUNK, GMLP_CHUNK), bool)), w_s, 0.0).astype(z.dtype)
    sg = jnp.einsum("gts,bcsgk->bctgk", w_causal, z) + b_s.T[:, :, None].astype(z.dtype)
    c_out = u * sg.reshape(B, S, D_GMLP)
    shp = (B, S, SB_HEADS, SB_HEAD_DIM)
    d_out = stick_breaking_attention(q.reshape(shp), k.reshape(shp), v.reshape(shp))
    d_out = d_out.reshape(B, S, D_SB).astype(h.dtype)
    return jnp.concatenate([c_out, d_out], axis=-1) @ w_out


def moe_swiglu(h, router_w, router_b, w1, w3, w2):
    logits = (h @ router_w + router_b).astype(jnp.float32)
    top_val, top_idx = lax.top_k(logits, TOP_K)
    top_gate = jax.nn.softmax(top_val, axis=-1)
    combine = jnp.sum(jax.nn.one_hot(top_idx, N_EXPERTS, dtype=jnp.float32) * top_gate[..., None], axis=-2)
    combine = combine.astype(h.dtype)
    out = jnp.zeros_like(h)
    for e in range(N_EXPERTS):
        out = out + combine[..., e:e + 1] * swiglu(h, w1[e], w3[e], w2[e])
    return out


def setup_inputs(seed: int = 0) -> dict:
    key = jax.random.key(seed)
    ks = iter(jax.random.split(key, 40))

    def nrm(shape, scale):
        return jax.random.normal(next(ks), shape, jnp.float32) * scale

    def gain(shape):
        return 1.0 + nrm(shape, 0.02)

    H = MLSTM_HEADS
    i_bias = nrm((N_EVEN, H), 0.1)
    f_bias = jnp.linspace(3.0, 6.0, H, dtype=jnp.float32)[None, :] + nrm((N_EVEN, H), 0.1)
    return {
        "x": nrm((BATCH, SEQ, D_MODEL), 1.0),
        "ab_w_in": nrm((N_EVEN, D_MODEL, IN_EVEN), D_MODEL ** -0.5),
        "ab_gate_bias": jnp.concatenate([i_bias, f_bias], axis=-1),
        "a_conv_w": nrm((N_EVEN, CONV_WIDTH, D_CONV), CONV_WIDTH ** -0.5),
        "a_conv_b": nrm((N_EVEN, D_CONV), 0.02),
        "a_norm_g": gain((N_EVEN, D_CONV)),
        "a_norm_b": nrm((N_EVEN, D_CONV), 0.02),
        "b_norm_g": gain((N_EVEN, D_MLSTM)),
        "ab_w_out": nrm((N_EVEN, D_MIX_EVEN, D_MODEL), BETA_INIT * D_MIX_EVEN ** -0.5),
        "ab_ln1_g": gain((N_EVEN, D_MODEL)),
        "ab_ln1_b": nrm((N_EVEN, D_MODEL), 0.02),
        "ffn_w1": nrm((N_EVEN, D_MODEL, D_FF), D_MODEL ** -0.5),
        "ffn_w3": nrm((N_EVEN, D_MODEL, D_FF), D_MODEL ** -0.5),
        "ffn_w2": nrm((N_EVEN, D_FF, D_MODEL), BETA_INIT * D_FF ** -0.5),
        "ab_ln2_g": gain((N_EVEN, D_MODEL)),
        "ab_ln2_b": nrm((N_EVEN, D_MODEL), 0.02),
        "cd_w_in": nrm((N_ODD, D_MODEL, IN_ODD), D_MODEL ** -0.5),
        "c_norm_g": gain((N_ODD, D_GMLP)),
        "c_norm_b": nrm((N_ODD, D_GMLP), 0.02),
        "c_w_s": nrm((N_ODD, GMLP_GROUPS, GMLP_CHUNK, GMLP_CHUNK), GMLP_CHUNK ** -0.5),
        "c_b_s": gain((N_ODD, GMLP_GROUPS, GMLP_CHUNK)),
        "cd_w_out": nrm((N_ODD, D_MIX_ODD, D_MODEL), BETA_INIT * D_MIX_ODD ** -0.5),
        "cd_ln1_g": gain((N_ODD, D_MODEL)),
        "cd_ln1_b": nrm((N_ODD, D_MODEL), 0.02),
        "router_w": nrm((N_ODD, D_MODEL, N_EXPERTS), D_MODEL ** -0.5),
        "router_b": nrm((N_ODD, N_EXPERTS), 0.01),
        "moe_w1": nrm((N_ODD, N_EXPERTS, D_MODEL, D_FF), D_MODEL ** -0.5),
        "moe_w3": nrm((N_ODD, N_EXPERTS, D_MODEL, D_FF), D_MODEL ** -0.5),
        "moe_w2": nrm((N_ODD, N_EXPERTS, D_FF, D_MODEL), BETA_INIT * D_FF ** -0.5),
        "cd_ln2_g": gain((N_ODD, D_MODEL)),
        "cd_ln2_b": nrm((N_ODD, D_MODEL), 0.02),
    }


def reference(x, ab_w_in, ab_gate_bias, a_conv_w, a_conv_b, a_norm_g, a_norm_b, b_norm_g, ab_w_out,
              ab_ln1_g, ab_ln1_b, ffn_w1, ffn_w3, ffn_w2, ab_ln2_g, ab_ln2_b,
              cd_w_in, c_norm_g, c_norm_b, c_w_s, c_b_s, cd_w_out, cd_ln1_g, cd_ln1_b,
              router_w, router_b, moe_w1, moe_w3, moe_w2, cd_ln2_g, cd_ln2_b):
    for layer in range(DEPTH):
        j = layer // 2
        if layer % 2 == 0:
            mix = conv_mlstm_mixer(x, ab_w_in[j], ab_gate_bias[j], a_conv_w[j], a_conv_b[j],
                                   a_norm_g[j], a_norm_b[j], b_norm_g[j], ab_w_out[j])
            x = layer_norm(ALPHA * x + mix, ab_ln1_g[j], ab_ln1_b[j])
            x = layer_norm(ALPHA * x + swiglu(x, ffn_w1[j], ffn_w3[j], ffn_w2[j]), ab_ln2_g[j], ab_ln2_b[j])
        else:
            mix = gmlp_sb_mixer(x, cd_w_in[j], c_norm_g[j], c_norm_b[j], c_w_s[j], c_b_s[j], cd_w_out[j])
            x = layer_norm(ALPHA * x + mix, cd_ln1_g[j], cd_ln1_b[j])
            ffn = moe_swiglu(x, router_w[j], router_b[j], moe_w1[j], moe_w3[j], moe_w2[j])
            x = layer_norm(ALPHA * x + ffn, cd_ln2_g[j], cd_ln2_b[j])
    return x
```

```python
import functools

import jax
import jax.numpy as jnp
from jax import lax
from jax.experimental import pallas as pl
from jax.experimental.pallas import tpu as pltpu

F32 = jnp.float32
BF16 = jnp.bfloat16

D_MODEL = 1024
DEPTH = 4
HALF = D_MODEL // 2
CONV_WIDTH = 31
CONV_HALO = 32
MLSTM_HEADS = 4
MLSTM_HEAD_DIM = HALF // MLSTM_HEADS
CHUNK = 128
GMLP_GROUPS = 4
SB_HEADS = 8
SB_HEAD_DIM = HALF // SB_HEADS
D_FF = 2816
FF_CHUNK = 256
N_FF_CHUNKS = D_FF // FF_CHUNK
N_EXPERTS = 8
ALPHA = (2 * DEPTH) ** 0.25
LN_EPS = 1e-5
LANES = 128
V7X_VMEM_LIMIT = 56 * 1024 * 1024

TOKEN_TILE = 512
MOE_TILE = 512
SB_TILE = 256


def _ln(x, g, b):
    mu = jnp.mean(x, axis=-1, keepdims=True)
    xc = x - mu
    var = jnp.mean(xc * xc, axis=-1, keepdims=True)
    return xc * lax.rsqrt(var + LN_EPS) * g + b


def _gelu(x):
    return 0.5 * x * (1.0 + lax.erf(x * (0.5 ** 0.5)))


def _dot(a, b):
    return jnp.dot(a, b, preferred_element_type=F32)


def _dot_nt(a, b):
    return lax.dot_general(a, b, (((1,), (1,)), ((), ())), preferred_element_type=F32)


def _dot_tn(a, b):
    return lax.dot_general(a, b, (((0,), (0,)), ((), ())), preferred_element_type=F32)


def _split_bf16(x):
    hi = x.astype(BF16)
    lo = (x - hi.astype(F32)).astype(BF16)
    return hi, lo


def _params(*sem):
    return pltpu.CompilerParams(dimension_semantics=sem, vmem_limit_bytes=V7X_VMEM_LIMIT)


def _row_spec(tm, width):
    return pl.BlockSpec((tm, width), lambda i: (i, 0))


def _full_spec(shape):
    zeros = (0,) * len(shape)
    return pl.BlockSpec(shape, lambda *_: zeros)


def _even_in_kernel(x_ref, w_ref, wg_ref, gb_ref, glu_ref, q_ref, k_ref, v_ref, o_ref, g_ref):
    xb = x_ref[...].astype(BF16)

    def proj(c):
        return _dot(xb, w_ref[:, c * HALF:(c + 1) * HALF])

    glu_ref[...] = proj(0) * jax.nn.sigmoid(proj(1))
    q_ref[...] = proj(2).astype(BF16)
    k_ref[...] = (proj(3) * MLSTM_HEAD_DIM ** -0.5).astype(BF16)
    v_ref[...] = proj(4).astype(BF16)
    o_ref[...] = proj(5).astype(BF16)
    g_ref[...] = _dot(xb, wg_ref[...]) + gb_ref[...]


def _even_in(x, w, wg, gb, tm):
    t = x.shape[0]
    half_bf = jax.ShapeDtypeStruct((t, HALF), BF16)
    return pl.pallas_call(
        _even_in_kernel,
        grid=(t // tm,),
        in_specs=[_row_spec(tm, D_MODEL), _full_spec(w.shape), _full_spec(wg.shape), _full_spec(gb.shape)],
        out_specs=[_row_spec(tm, HALF)] * 5 + [_row_spec(tm, LANES)],
        out_shape=[jax.ShapeDtypeStruct((t, HALF), F32), half_bf, half_bf, half_bf, half_bf,
                   jax.ShapeDtypeStruct((t, LANES), F32)],
        compiler_params=_params("parallel"),
        name="even_in_proj",
    )(x, w, wg, gb)


def _conv_kernel(prev_ref, cur_ref, w_ref, cb_ref, g_ref, b_ref, out_ref, win_ref, y_ref):
    ts = cur_ref.shape[1]
    first = pl.program_id(1) == 0
    win_ref[0:CONV_HALO, :] = jnp.where(first, 0.0, prev_ref[0])
    win_ref[CONV_HALO:, :] = cur_ref[0]
    lead = CONV_HALO - (CONV_WIDTH - 1)
    for cb in range(HALF // LANES):
        lanes = slice(cb * LANES, (cb + 1) * LANES)
        for rb in range(ts // CHUNK):
            acc = jnp.broadcast_to(cb_ref[:, lanes], (CHUNK, LANES))
            for r in range(8):
                offs = [o for o in range(lead, lead + CONV_WIDTH) if o % 8 == r]
                a_min, a_max = min(offs) // 8, max(offs) // 8
                slab = win_ref[pl.ds(rb * CHUNK + r + 8 * a_min, CHUNK + 8 * (a_max - a_min)), lanes]
                for o in offs:
                    a = o // 8 - a_min
                    j = o - lead
                    acc = acc + w_ref[j:j + 1, lanes] * slab[8 * a:8 * a + CHUNK, :]
            y_ref[rb * CHUNK:(rb + 1) * CHUNK, lanes] = acc
    yn = _ln(y_ref[...], g_ref[...], b_ref[...])
    out_ref[0] = (yn * jax.nn.sigmoid(yn)).astype(BF16)


def _conv_branch(glu, w, cb, g, b, ts):
    bsz, s, _ = glu.shape
    per = ts // CONV_HALO
    return pl.pallas_call(
        _conv_kernel,
        grid=(bsz, s // ts),
        in_specs=[
            pl.BlockSpec((1, CONV_HALO, HALF), lambda bi, i: (bi, jnp.maximum(i * per - 1, 0), 0)),
            pl.BlockSpec((1, ts, HALF), lambda bi, i: (bi, i, 0)),
            _full_spec(w.shape), _full_spec(cb.shape), _full_spec(g.shape), _full_spec(b.shape),
        ],
        out_specs=pl.BlockSpec((1, ts, HALF), lambda bi, i: (bi, i, 0)),
        out_shape=jax.ShapeDtypeStruct((bsz, s, HALF), BF16),
        scratch_shapes=[pltpu.VMEM((ts + CONV_HALO, HALF), F32), pltpu.VMEM((ts, HALF), F32)],
        compiler_params=_params("parallel", "parallel"),
        name="conv_branch",
    )(glu, glu, w, cb, g, b)


def _mlstm_kernel(q_ref, k_ref, v_ref, o_ref, g_ref, hg_ref, out_ref, c_ref, n_ref, m_ref):
    L, H, d = CHUNK, MLSTM_HEADS, MLSTM_HEAD_DIM

    @pl.when(pl.program_id(1) == 0)
    def _():
        c_ref[...] = jnp.zeros_like(c_ref)
        n_ref[...] = jnp.zeros_like(n_ref)
        m_ref[...] = jnp.zeros_like(m_ref)

    gates = g_ref[0]
    logf = jax.nn.log_sigmoid(gates)
    row = lax.broadcasted_iota(jnp.int32, (L, L), 0)
    col = lax.broadcasted_iota(jnp.int32, (L, L), 1)
    causal = col <= row
    tri = jnp.where(causal, 1.0, 0.0).astype(BF16)
    lf_hi, lf_lo = _split_bf16(logf)
    bcum = _dot(tri, lf_hi) + _dot(tri, lf_lo)
    gates_t = gates.T
    bcum_t = bcum.T

    for h in range(H):
        hs = slice(h * d, (h + 1) * d)
        qh, kh, vh = q_ref[0, :, hs], k_ref[0, :, hs], v_ref[0, :, hs]
        i_c = gates[:, h:h + 1]
        b_c = bcum[:, H + h:H + h + 1]
        i_r = gates_t[h:h + 1, :]
        b_r = bcum_t[H + h:H + h + 1, :]
        gtot = bcum[L - 1:L, H + h:H + h + 1]

        c_prev = c_ref[h]
        n_prev = n_ref[h, 0:1, :]
        m_prev = m_ref[h, 0:1, 0:1]

        log_w = gtot - b_c + i_c
        a = jnp.max(log_w, axis=0, keepdims=True)
        kw = kh.astype(F32) * jnp.exp(log_w - a)
        kv_loc = _dot_tn(kw.astype(BF16), vh)
        n_loc = jnp.sum(kw, axis=0, keepdims=True)

        m_new = jnp.maximum(gtot + m_prev, a)
        s_old = jnp.exp(gtot + m_prev - m_new)
        s_new = jnp.exp(a - m_new)
        c_ref[h] = s_old * c_prev + s_new * kv_loc
        n_ref[h] = jnp.broadcast_to(s_old * n_prev + s_new * n_loc, (8, d))
        m_ref[h] = jnp.broadcast_to(m_new, (8, LANES))

        log_d = jnp.where(causal, b_c - b_r + i_r, -jnp.inf)
        log_inter = b_c + m_prev
        m_t = jnp.maximum(log_inter, jnp.max(log_d, axis=1, keepdims=True))
        d_intra = jnp.exp(log_d - m_t)
        s_inter = jnp.exp(log_inter - m_t)
        qk = _dot_nt(qh, kh) * d_intra
        num = _dot(qk.astype(BF16), vh) + s_inter * _dot(qh, c_prev.astype(BF16))
        den = (jnp.sum(qk, axis=1, keepdims=True)
               + s_inter * jnp.sum(qh.astype(F32) * n_prev, axis=1, keepdims=True))
        hh = num / jnp.maximum(jnp.abs(den), jnp.exp(-m_t))

        mu = jnp.mean(hh, axis=1, keepdims=True)
        hc = hh - mu
        var = jnp.mean(hc * hc, axis=1, keepdims=True)
        hn = hc * lax.rsqrt(var + LN_EPS) * hg_ref[:, hs]
        out_ref[0, :, hs] = (jax.nn.sigmoid(o_ref[0, :, hs].astype(F32)) * hn).astype(BF16)


def _mlstm_branch(q, k, v, o, gates, head_g):
    bsz, s, _ = q.shape
    blk = pl.BlockSpec((1, CHUNK, HALF), lambda bi, c: (bi, c, 0))
    return pl.pallas_call(
        _mlstm_kernel,
        grid=(bsz, s // CHUNK),
        in_specs=[blk, blk, blk, blk, pl.BlockSpec((1, CHUNK, LANES), lambda bi, c: (bi, c, 0)),
                  _full_spec(head_g.shape)],
        out_specs=blk,
        out_shape=jax.ShapeDtypeStruct((bsz, s, HALF), BF16),
        scratch_shapes=[pltpu.VMEM((MLSTM_HEADS, MLSTM_HEAD_DIM, MLSTM_HEAD_DIM), F32),
                        pltpu.VMEM((MLSTM_HEADS, 8, MLSTM_HEAD_DIM), F32),
                        pltpu.VMEM((MLSTM_HEADS, 8, LANES), F32)],
        compiler_params=_params("parallel", "arbitrary"),
        name="mlstm_branch",
    )(q, k, v, o, gates, head_g)


def _out_ln_kernel(a_ref, b_ref, x_ref, w_ref, g_ref, beta_ref, out_ref):
    mix = _dot(a_ref[...], w_ref[0:HALF, :]) + _dot(b_ref[...], w_ref[HALF:, :])
    out_ref[...] = _ln(ALPHA * x_ref[...] + mix, g_ref[...], beta_ref[...])


def _out_ln(a, b, x, w, g, beta, tm):
    t = x.shape[0]
    return pl.pallas_call(
        _out_ln_kernel,
        grid=(t // tm,),
        in_specs=[_row_spec(tm, HALF), _row_spec(tm, HALF), _row_spec(tm, D_MODEL),
                  _full_spec(w.shape), _full_spec(g.shape), _full_spec(beta.shape)],
        out_specs=_row_spec(tm, D_MODEL),
        out_shape=jax.ShapeDtypeStruct((t, D_MODEL), F32),
        compiler_params=_params("parallel"),
        name="out_proj_ln",
    )(a, b, x, w, g, beta)


def _swiglu_into(xb_ref, w1_ref, w3_ref, w2_ref, acc_ref):
    acc_ref[...] = jnp.zeros_like(acc_ref)

    def body(c, carry):
        xb = xb_ref[...]
        h1 = _dot(xb, w1_ref[c])
        h3 = _dot(xb, w3_ref[c])
        hh = (h1 * jax.nn.sigmoid(h1) * h3).astype(BF16)
        acc_ref[...] += _dot(hh, w2_ref[c])
        return carry

    lax.fori_loop(0, N_FF_CHUNKS, body, 0)


def _ffn_kernel(x_ref, w1_ref, w3_ref, w2_ref, g_ref, beta_ref, out_ref, xb_ref, acc_ref):
    xb_ref[...] = x_ref[...].astype(BF16)
    _swiglu_into(xb_ref, w1_ref, w3_ref, w2_ref, acc_ref)
    out_ref[...] = _ln(ALPHA * x_ref[...] + acc_ref[...], g_ref[...], beta_ref[...])


def _ffn(x, w1, w3, w2, g, beta, tm):
    t = x.shape[0]
    return pl.pallas_call(
        _ffn_kernel,
        grid=(t // tm,),
        in_specs=[_row_spec(tm, D_MODEL), _full_spec(w1.shape), _full_spec(w3.shape), _full_spec(w2.shape),
                  _full_spec(g.shape), _full_spec(beta.shape)],
        out_specs=_row_spec(tm, D_MODEL),
        out_shape=jax.ShapeDtypeStruct((t, D_MODEL), F32),
        scratch_shapes=[pltpu.VMEM((tm, D_MODEL), BF16), pltpu.VMEM((tm, D_MODEL), F32)],
        compiler_params=_params("parallel"),
        name="dense_swiglu_ln",
    )(x, w1, w3, w2, g, beta)


def _odd_in_kernel(x_ref, w_ref, vg_ref, vb_ref, ws_ref, bs_ref, c_ref, q_ref, k_ref, v_ref):
    tm = x_ref.shape[0]
    xb = x_ref[...].astype(BF16)

    def proj(c):
        return _dot(xb, w_ref[:, c * HALF:(c + 1) * HALF])

    u = _gelu(proj(0))
    zn = _ln(_gelu(proj(1)), vg_ref[...], vb_ref[...]).astype(BF16)
    q_ref[...] = (proj(2) * SB_HEAD_DIM ** -0.5).astype(BF16)
    k_ref[...] = proj(3).astype(BF16)
    v_ref[...] = proj(4).astype(BF16)

    row = lax.broadcasted_iota(jnp.int32, (CHUNK, CHUNK), 0)
    col = lax.broadcasted_iota(jnp.int32, (CHUNK, CHUNK), 1)
    for g in range(GMLP_GROUPS):
        lanes = slice(g * LANES, (g + 1) * LANES)
        wc = jnp.where(col <= row, ws_ref[g], 0.0).astype(BF16)
        bias = bs_ref[:, g:g + 1]
        for r in range(tm // CHUNK):
            rows = slice(r * CHUNK, (r + 1) * CHUNK)
            sg = _dot(wc, zn[rows, lanes]) + bias
            c_ref[rows, lanes] = (u[rows, lanes] * sg).astype(BF16)


def _odd_in(x, w, vg, vb, ws, bs_t, tm):
    t = x.shape[0]
    half_bf = jax.ShapeDtypeStruct((t, HALF), BF16)
    return pl.pallas_call(
        _odd_in_kernel,
        grid=(t // tm,),
        in_specs=[_row_spec(tm, D_MODEL), _full_spec(w.shape), _full_spec(vg.shape), _full_spec(vb.shape),
                  _full_spec(ws.shape), _full_spec(bs_t.shape)],
        out_specs=[_row_spec(tm, HALF)] * 4,
        out_shape=[half_bf] * 4,
        compiler_params=_params("parallel"),
        name="odd_in_proj_gmlp",
    )(x, w, vg, vb, ws, bs_t)


def _sb_kernel(q_ref, k_ref, v_ref, out_ref):
    tq = q_ref.shape[1]
    tk = tq
    i = pl.program_id(2)
    q = q_ref[0]
    lane = lax.broadcasted_iota(jnp.int32, (tq, LANES), 1)
    zero = jnp.zeros_like(q)
    q_heads = (jnp.where(lane < SB_HEAD_DIM, q, zero), jnp.where(lane >= SB_HEAD_DIM, q, zero))
    row = lax.broadcasted_iota(jnp.int32, (tq, tk), 0)
    col = lax.broadcasted_iota(jnp.int32, (tq, tk), 1)
    suffix = jnp.where(row >= col, 1.0, 0.0).astype(BF16)
    strictly_before = col < row

    def block(j, carry, diagonal):
        kb = k_ref[0, pl.ds(pl.multiple_of(j * tk, tk), tk), :]
        vb = v_ref[0, pl.ds(pl.multiple_of(j * tk, tk), tk), :]
        new = []
        for qm, (run, acc) in zip(q_heads, carry):
            z = _dot_nt(qm, kb)
            log_1mb = -(jnp.maximum(z, 0.0) + jnp.log1p(jnp.exp(-jnp.abs(z))))
            if diagonal:
                log_1mb = jnp.where(strictly_before, log_1mb, 0.0)
            hi, lo = _split_bf16(log_1mb)
            incl = _dot(hi, suffix) + _dot(lo, suffix)
            att = jnp.exp(z + incl + run)
            if diagonal:
                att = jnp.where(strictly_before, att, 0.0)
            acc = acc + _dot(att.astype(BF16), vb)
            run = run + jnp.sum(log_1mb, axis=1, keepdims=True)
            new.append((run, acc))
        return tuple(new)

    init = tuple((jnp.zeros((tq, 1), F32), jnp.zeros((tq, LANES), F32)) for _ in q_heads)
    carry = block(i, init, True)
    carry = lax.fori_loop(0, i, lambda n, cr: block(i - 1 - n, cr, False), carry)
    out_ref[0] = jnp.where(lane < SB_HEAD_DIM, carry[0][1], carry[1][1]).astype(BF16)


def _sb_attention(q, k, v, tq):
    bsz, s, _ = q.shape
    kv_spec = pl.BlockSpec((1, s, LANES), lambda bi, p, i: (bi, 0, p))
    q_spec = pl.BlockSpec((1, tq, LANES), lambda bi, p, i: (bi, i, p))
    return pl.pallas_call(
        _sb_kernel,
        grid=(bsz, HALF // LANES, s // tq),
        in_specs=[q_spec, kv_spec, kv_spec],
        out_specs=q_spec,
        out_shape=jax.ShapeDtypeStruct((bsz, s, HALF), BF16),
        compiler_params=_params("parallel", "parallel", "arbitrary"),
        name="stick_breaking_attention",
    )(q, k, v)


def _odd_out_kernel(c_ref, d_ref, x_ref, w_ref, g_ref, beta_ref, rw_ref, rb_ref,
                    x1_ref, gate_ref, rank_ref, cnt_ref, base_ref):
    tm = x_ref.shape[0]

    @pl.when(pl.program_id(0) == 0)
    def _():
        base_ref[...] = jnp.zeros_like(base_ref)

    mix = _dot(c_ref[...], w_ref[0:HALF, :]) + _dot(d_ref[...], w_ref[HALF:, :])
    x1 = _ln(ALPHA * x_ref[...] + mix, g_ref[...], beta_ref[...])
    x1_ref[...] = x1

    xh, xl = _split_bf16(x1)
    logits = _dot(xh, rw_ref[0]) + _dot(xl, rw_ref[0]) + _dot(xh, rw_ref[1]) + rb_ref[...]
    lane = lax.broadcasted_iota(jnp.int32, (tm, LANES), 1)
    logits = jnp.where(lane < N_EXPERTS, logits, -jnp.inf)
    m1 = jnp.max(logits, axis=1, keepdims=True)
    i1 = jnp.min(jnp.where(logits == m1, lane, LANES), axis=1, keepdims=True)
    sel1 = lane == i1
    rest = jnp.where(sel1, -jnp.inf, logits)
    m2 = jnp.max(rest, axis=1, keepdims=True)
    i2 = jnp.min(jnp.where(rest == m2, lane, LANES), axis=1, keepdims=True)
    sel2 = lane == i2
    e = jnp.exp(m2 - m1)
    g1 = 1.0 / (1.0 + e)
    g2 = e / (1.0 + e)
    first_lower = i1 < i2
    gate_ref[...] = jnp.where(lane == 0, jnp.where(first_lower, g1, g2),
                              jnp.where(lane == 1, jnp.where(first_lower, g2, g1), 0.0))

    sel = sel1 | sel2
    sel_f = jnp.where(sel, 1.0, 0.0)
    row = lax.broadcasted_iota(jnp.int32, (tm, tm), 0)
    col = lax.broadcasted_iota(jnp.int32, (tm, tm), 1)
    before = jnp.where(col < row, 1.0, 0.0).astype(BF16)
    rank = base_ref[0:1, :] + _dot(before, sel_f.astype(BF16))
    rank_ref[...] = jnp.where(sel, rank, -1.0).astype(jnp.int32)
    base_ref[...] = base_ref[...] + jnp.sum(sel_f, axis=0, keepdims=True)
    cnt_ref[...] = base_ref[...].astype(jnp.int32)


def _odd_out(c, d, x, w, g, beta, rw, rb, tm):
    t = x.shape[0]
    lane_spec = _row_spec(tm, LANES)
    return pl.pallas_call(
        _odd_out_kernel,
        grid=(t // tm,),
        in_specs=[_row_spec(tm, HALF), _row_spec(tm, HALF), _row_spec(tm, D_MODEL),
                  _full_spec(w.shape), _full_spec(g.shape), _full_spec(beta.shape),
                  _full_spec(rw.shape), _full_spec(rb.shape)],
        out_specs=[_row_spec(tm, D_MODEL), lane_spec, lane_spec, _full_spec((8, LANES))],
        out_shape=[jax.ShapeDtypeStruct((t, D_MODEL), F32), jax.ShapeDtypeStruct((t, LANES), F32),
                   jax.ShapeDtypeStruct((t, LANES), jnp.int32), jax.ShapeDtypeStruct((8, LANES), jnp.int32)],
        scratch_shapes=[pltpu.VMEM((8, LANES), F32)],
        compiler_params=_params("arbitrary"),
        name="odd_out_proj_ln_router",
    )(c, d, x, w, g, beta, rw, rb)


def _moe_kernel(te_ref, nv_ref, meta_ref, x_hbm, w1_ref, w3_ref, w2_ref, out_hbm,
                xbuf, ybuf, xb_ref, gsem, ssem, *, n_tokens):
    tm = xbuf.shape[0]
    n_valid = nv_ref[pl.program_id(0)]

    def gather_copy(r):
        tok = meta_ref[0, 0, r] & (n_tokens - 1)
        return pltpu.make_async_copy(x_hbm.at[pl.ds(tok, 1)], xbuf.at[pl.ds(r, 1)], gsem)

    def scatter_copy(r):
        dest = meta_ref[0, 0, r]
        return pltpu.make_async_copy(ybuf.at[pl.ds(r, 1)], out_hbm.at[pl.ds(dest, 1)], ssem)

    def for_rows(n, fn):
        def body(r, carry):
            fn(r)
            return carry
        lax.fori_loop(0, n, body, 0)

    @pl.when(n_valid > 0)
    def _():
        for_rows(tm, lambda r: gather_copy(r).start())
        for_rows(tm, lambda r: gather_copy(r).wait())
        xb_ref[...] = xbuf[...].astype(BF16)
        _swiglu_into(xb_ref, w1_ref.at[0], w3_ref.at[0], w2_ref.at[0], ybuf)
        for_rows(n_valid, lambda r: scatter_copy(r).start())
        for_rows(n_valid, lambda r: scatter_copy(r).wait())


def _moe(x1, meta, tile_expert, n_valid, w1, w3, w2, tm):
    t = x1.shape[0]
    n_tiles = meta.shape[0]
    assert t & (t - 1) == 0, "token count must be a power of two (source token = dest & (t - 1))"
    w_spec = lambda shape: pl.BlockSpec((1,) + shape[1:], lambda i, te, nv: (te[i], 0, 0, 0))
    grid_spec = pltpu.PrefetchScalarGridSpec(
        num_scalar_prefetch=2,
        grid=(n_tiles,),
        in_specs=[
            pl.BlockSpec((1, 1, tm), lambda i, te, nv: (i, 0, 0), memory_space=pltpu.SMEM),
            pl.BlockSpec(memory_space=pl.ANY),
            w_spec(w1.shape), w_spec(w3.shape), w_spec(w2.shape),
        ],
        out_specs=pl.BlockSpec(memory_space=pl.ANY),
        scratch_shapes=[pltpu.VMEM((tm, D_MODEL), F32), pltpu.VMEM((tm, D_MODEL), F32),
                        pltpu.VMEM((tm, D_MODEL), BF16),
                        pltpu.SemaphoreType.DMA(()), pltpu.SemaphoreType.DMA(())],
    )
    return pl.pallas_call(
        functools.partial(_moe_kernel, n_tokens=t),
        grid_spec=grid_spec,
        out_shape=jax.ShapeDtypeStruct((2 * t, D_MODEL), F32),
        compiler_params=_params("arbitrary"),
        name="moe_grouped_swiglu",
    )(tile_expert, n_valid, meta, x1, w1, w3, w2)


def _moe_ln_kernel(x_ref, y0_ref, y1_ref, gate_ref, g_ref, beta_ref, out_ref):
    gate = gate_ref[...]
    ffn = gate[:, 0:1] * y0_ref[...] + gate[:, 1:2] * y1_ref[...]
    out_ref[...] = _ln(ALPHA * x_ref[...] + ffn, g_ref[...], beta_ref[...])


def _moe_ln(x1, y, gate, g, beta, tm):
    t = x1.shape[0]
    nt = t // tm
    return pl.pallas_call(
        _moe_ln_kernel,
        grid=(nt,),
        in_specs=[_row_spec(tm, D_MODEL), _row_spec(tm, D_MODEL),
                  pl.BlockSpec((tm, D_MODEL), lambda i: (i + nt, 0)),
                  _row_spec(tm, LANES), _full_spec(g.shape), _full_spec(beta.shape)],
        out_specs=_row_spec(tm, D_MODEL),
        out_shape=jax.ShapeDtypeStruct((t, D_MODEL), F32),
        compiler_params=_params("parallel"),
        name="moe_combine_ln",
    )(x1, y, y, gate, g, beta)


def _routing_tables(rank, cnt, t, tm):
    n_tiles = (2 * t) // tm + N_EXPERTS
    n_rows = n_tiles * tm
    rank8 = rank[:, :N_EXPERTS]
    sel = rank8 >= 0
    counts = cnt[0, :N_EXPERTS]
    padded = ((counts + tm - 1) // tm) * tm
    ends = jnp.cumsum(padded)
    starts = ends - padded
    pos = jnp.where(sel, starts[None, :] + rank8, n_rows)
    slot = jnp.cumsum(sel.astype(jnp.int32), axis=1) - 1
    dest = slot * t + jnp.arange(t, dtype=jnp.int32)[:, None]
    meta = jnp.zeros((n_rows,), jnp.int32).at[pos.reshape(-1)].set(dest.reshape(-1), mode="drop")
    tile_start = jnp.arange(n_tiles, dtype=jnp.int32) * tm
    tile_expert = jnp.minimum(jnp.searchsorted(ends, tile_start, side="right"), N_EXPERTS - 1).astype(jnp.int32)
    n_valid = jnp.clip((starts + counts)[tile_expert] - tile_start, 0, tm).astype(jnp.int32)
    return meta.reshape(n_tiles, 1, tm), tile_expert, n_valid


def _row(v):
    return v.reshape(1, -1).astype(F32)


def _ff_chunks_in(w):
    lead = w.shape[:-2]
    w = w.astype(BF16).reshape(lead + (D_MODEL, N_FF_CHUNKS, FF_CHUNK))
    return jnp.swapaxes(w, -3, -2)


def _ff_chunks_out(w):
    return w.astype(BF16).reshape(w.shape[:-2] + (N_FF_CHUNKS, FF_CHUNK, D_MODEL))


def _even_layer(x, bsz, s, w_in, gate_bias, conv_w, conv_b, norm_g, norm_b, head_g, w_out,
                ln1_g, ln1_b, w1, w3, w2, ln2_g, ln2_b):
    main = 6 * HALF
    w_main = w_in[:, :main].astype(BF16)
    w_gate = jnp.pad(w_in[:, main:], ((0, 0), (0, LANES - 2 * MLSTM_HEADS))).astype(BF16)
    gb = jnp.pad(gate_bias, (0, LANES - 2 * MLSTM_HEADS)).reshape(1, LANES)
    glu, q, k, v, o, gates = _even_in(x, w_main, w_gate, gb, TOKEN_TILE)
    shp = lambda a: a.reshape(bsz, s, a.shape[-1])
    conv_w32 = jnp.pad(conv_w, ((0, 32 - CONV_WIDTH), (0, 0)))
    a = _conv_branch(shp(glu), conv_w32, _row(conv_b), _row(norm_g), _row(norm_b), min(s, 512))
    hb = _mlstm_branch(shp(q), shp(k), shp(v), shp(o), shp(gates), _row(head_g))
    x1 = _out_ln(a.reshape(-1, HALF), hb.reshape(-1, HALF), x, w_out.astype(BF16), _row(ln1_g), _row(ln1_b),
                 TOKEN_TILE)
    return _ffn(x1, _ff_chunks_in(w1), _ff_chunks_in(w3), _ff_chunks_out(w2), _row(ln2_g), _row(ln2_b), TOKEN_TILE)


def _odd_layer(x, bsz, s, w_in, v_g, v_b, w_s, b_s, w_out, ln1_g, ln1_b, router_w, router_b,
               w1, w3, w2, ln2_g, ln2_b):
    t = x.shape[0]
    c, q, k, v = _odd_in(x, w_in.astype(BF16), _row(v_g), _row(v_b), w_s, b_s.T, TOKEN_TILE)
    shp = lambda a: a.reshape(bsz, s, HALF)
    d = _sb_attention(shp(q), shp(k), shp(v), min(s, SB_TILE)).reshape(t, HALF)
    rw = jnp.pad(router_w, ((0, 0), (0, LANES - N_EXPERTS)))
    rw_hi = rw.astype(BF16)
    rw_lo = (rw - rw_hi.astype(F32)).astype(BF16)
    rb = jnp.pad(router_b, (0, LANES - N_EXPERTS)).reshape(1, LANES)
    x1, gate, rank, cnt = _odd_out(c, d, x, w_out.astype(BF16), _row(ln1_g), _row(ln1_b),
                                   jnp.stack([rw_hi, rw_lo]), rb, TOKEN_TILE)
    meta, tile_expert, n_valid = _routing_tables(rank, cnt, t, MOE_TILE)
    y = _moe(x1, meta, tile_expert, n_valid, _ff_chunks_in(w1), _ff_chunks_in(w3), _ff_chunks_out(w2), MOE_TILE)
    return _moe_ln(x1, y, gate, _row(ln2_g), _row(ln2_b), TOKEN_TILE)


def kernel(x, ab_w_in, ab_gate_bias, a_conv_w, a_conv_b, a_norm_g, a_norm_b, b_norm_g, ab_w_out, ab_ln1_g, ab_ln1_b, ffn_w1, ffn_w3, ffn_w2, ab_ln2_g, ab_ln2_b, cd_w_in, c_norm_g, c_norm_b, c_w_s, c_b_s, cd_w_out, cd_ln1_g, cd_ln1_b, router_w, router_b, moe_w1, moe_w3, moe_w2, cd_ln2_g, cd_ln2_b):
    bsz, s, _ = x.shape
    h = x.reshape(bsz * s, D_MODEL)
    for layer in range(DEPTH):
        j = layer // 2
        if layer % 2 == 0:
            h = _even_layer(h, bsz, s, ab_w_in[j], ab_gate_bias[j], a_conv_w[j], a_conv_b[j], a_norm_g[j],
                            a_norm_b[j], b_norm_g[j], ab_w_out[j], ab_ln1_g[j], ab_ln1_b[j],
                            ffn_w1[j], ffn_w3[j], ffn_w2[j], ab_ln2_g[j], ab_ln2_b[j])
        else:
            h = _odd_layer(h, bsz, s, cd_w_in[j], c_norm_g[j], c_norm_b[j], c_w_s[j], c_b_s[j], cd_w_out[j],
                           cd_ln1_g[j], cd_ln1_b[j], router_w[j], router_b[j],
                           moe_w1[j], moe_w3[j], moe_w2[j], cd_ln2_g[j], cd_ln2_b[j])
    return h.reshape(bsz, s, D_MODEL)
```

```python
import functools

import jax
import jax.numpy as jnp
from jax import lax
from jax.experimental import pallas as pl
from jax.experimental.pallas import tpu as pltpu

F32 = jnp.float32
BF16 = jnp.bfloat16

D_MODEL = 1024
DEPTH = 4
HALF = D_MODEL // 2
CONV_WIDTH = 31
CONV_HALO = 32
MLSTM_HEADS = 4
MLSTM_HEAD_DIM = HALF // MLSTM_HEADS
CHUNK = 128
GMLP_GROUPS = 4
SB_HEADS = 8
SB_HEAD_DIM = HALF // SB_HEADS
D_FF = 2816
FF_CHUNK = 256
N_FF_CHUNKS = D_FF // FF_CHUNK
N_EXPERTS = 8
ALPHA = (2 * DEPTH) ** 0.25
LN_EPS = 1e-5
LOG2E = 1.4426950408889634
MASKED_SCORE = -1e30
LANES = 128
V7X_VMEM_LIMIT = 56 * 1024 * 1024

TOKEN_TILE = 512
MOE_TILE = 512
SB_TILE = 512
SB_KEY_TILE = 256


def _ln(x, g, b):
    mu = jnp.mean(x, axis=-1, keepdims=True)
    xc = x - mu
    var = jnp.mean(xc * xc, axis=-1, keepdims=True)
    return xc * lax.rsqrt(var + LN_EPS) * g + b


def _gelu(x):
    return 0.5 * x * (1.0 + lax.erf(x * (0.5 ** 0.5)))


def _dot(a, b):
    return jnp.dot(a, b, preferred_element_type=F32)


def _dot_nt(a, b):
    return lax.dot_general(a, b, (((1,), (1,)), ((), ())), preferred_element_type=F32)


def _dot_tn(a, b):
    return lax.dot_general(a, b, (((0,), (0,)), ((), ())), preferred_element_type=F32)


def _split_bf16(x):
    hi = x.astype(BF16)
    lo = (x - hi.astype(F32)).astype(BF16)
    return hi, lo


def _params(*sem):
    return pltpu.CompilerParams(dimension_semantics=sem, vmem_limit_bytes=V7X_VMEM_LIMIT)


def _row_spec(tm, width):
    return pl.BlockSpec((tm, width), lambda i: (i, 0))


def _full_spec(shape):
    zeros = (0,) * len(shape)
    return pl.BlockSpec(shape, lambda *_: zeros)


def _even_in_kernel(x_ref, w_ref, wg_ref, gb_ref, glu_ref, q_ref, k_ref, v_ref, o_ref, g_ref):
    xb = x_ref[...].astype(BF16)

    def proj(c):
        return _dot(xb, w_ref[:, c * HALF:(c + 1) * HALF])

    glu_ref[...] = proj(0) * jax.nn.sigmoid(proj(1))
    q_ref[...] = proj(2).astype(BF16)
    k_ref[...] = (proj(3) * MLSTM_HEAD_DIM ** -0.5).astype(BF16)
    v_ref[...] = proj(4).astype(BF16)
    o_ref[...] = proj(5).astype(BF16)
    g_ref[...] = _dot(xb, wg_ref[...]) + gb_ref[...]


def _even_in(x, w, wg, gb, tm):
    t = x.shape[0]
    half_bf = jax.ShapeDtypeStruct((t, HALF), BF16)
    return pl.pallas_call(
        _even_in_kernel,
        grid=(t // tm,),
        in_specs=[_row_spec(tm, D_MODEL), _full_spec(w.shape), _full_spec(wg.shape), _full_spec(gb.shape)],
        out_specs=[_row_spec(tm, HALF)] * 5 + [_row_spec(tm, LANES)],
        out_shape=[jax.ShapeDtypeStruct((t, HALF), F32), half_bf, half_bf, half_bf, half_bf,
                   jax.ShapeDtypeStruct((t, LANES), F32)],
        compiler_params=_params("parallel"),
        name="even_in_proj",
    )(x, w, wg, gb)


def _conv_kernel(prev_ref, cur_ref, w_ref, cb_ref, g_ref, b_ref, out_ref, win_ref, y_ref):
    ts = cur_ref.shape[1]
    first = pl.program_id(1) == 0
    win_ref[0:CONV_HALO, :] = jnp.where(first, 0.0, prev_ref[0])
    win_ref[CONV_HALO:, :] = cur_ref[0]
    lead = CONV_HALO - (CONV_WIDTH - 1)
    for cb in range(HALF // LANES):
        lanes = slice(cb * LANES, (cb + 1) * LANES)
        for rb in range(ts // CHUNK):
            acc = jnp.broadcast_to(cb_ref[:, lanes], (CHUNK, LANES))
            for r in range(8):
                offs = [o for o in range(lead, lead + CONV_WIDTH) if o % 8 == r]
                a_min, a_max = min(offs) // 8, max(offs) // 8
                slab = win_ref[pl.ds(rb * CHUNK + r + 8 * a_min, CHUNK + 8 * (a_max - a_min)), lanes]
                for o in offs:
                    a = o // 8 - a_min
                    j = o - lead
                    acc = acc + w_ref[j:j + 1, lanes] * slab[8 * a:8 * a + CHUNK, :]
            y_ref[rb * CHUNK:(rb + 1) * CHUNK, lanes] = acc
    yn = _ln(y_ref[...], g_ref[...], b_ref[...])
    out_ref[0] = (yn * jax.nn.sigmoid(yn)).astype(BF16)


def _conv_branch(glu, w, cb, g, b, ts):
    bsz, s, _ = glu.shape
    per = ts // CONV_HALO
    return pl.pallas_call(
        _conv_kernel,
        grid=(bsz, s // ts),
        in_specs=[
            pl.BlockSpec((1, CONV_HALO, HALF), lambda bi, i: (bi, jnp.maximum(i * per - 1, 0), 0)),
            pl.BlockSpec((1, ts, HALF), lambda bi, i: (bi, i, 0)),
            _full_spec(w.shape), _full_spec(cb.shape), _full_spec(g.shape), _full_spec(b.shape),
        ],
        out_specs=pl.BlockSpec((1, ts, HALF), lambda bi, i: (bi, i, 0)),
        out_shape=jax.ShapeDtypeStruct((bsz, s, HALF), BF16),
        scratch_shapes=[pltpu.VMEM((ts + CONV_HALO, HALF), F32), pltpu.VMEM((ts, HALF), F32)],
        compiler_params=_params("parallel", "parallel"),
        name="conv_branch",
    )(glu, glu, w, cb, g, b)


def _mlstm_kernel(q_ref, k_ref, v_ref, o_ref, g_ref, hg_ref, out_ref, c_ref, n_ref, m_ref):
    L, H, d = CHUNK, MLSTM_HEADS, MLSTM_HEAD_DIM

    @pl.when(pl.program_id(1) == 0)
    def _():
        c_ref[...] = jnp.zeros_like(c_ref)
        n_ref[...] = jnp.zeros_like(n_ref)
        m_ref[...] = jnp.zeros_like(m_ref)

    gates = g_ref[0]
    logf = jax.nn.log_sigmoid(gates)
    row = lax.broadcasted_iota(jnp.int32, (L, L), 0)
    col = lax.broadcasted_iota(jnp.int32, (L, L), 1)
    causal = col <= row
    tri = jnp.where(causal, 1.0, 0.0).astype(BF16)
    lf_hi, lf_lo = _split_bf16(logf)
    bcum = _dot(tri, lf_hi) + _dot(tri, lf_lo)
    gates_t = gates.T
    bcum_t = bcum.T

    for h in range(H):
        hs = slice(h * d, (h + 1) * d)
        qh, kh, vh = q_ref[0, :, hs], k_ref[0, :, hs], v_ref[0, :, hs]
        i_c = gates[:, h:h + 1]
        b_c = bcum[:, H + h:H + h + 1]
        i_r = gates_t[h:h + 1, :]
        b_r = bcum_t[H + h:H + h + 1, :]
        gtot = bcum[L - 1:L, H + h:H + h + 1]

        c_prev = c_ref[h]
        n_prev = n_ref[h, 0:1, :]
        m_prev = m_ref[h, 0:1, 0:1]

        log_w = gtot - b_c + i_c
        a = jnp.max(log_w, axis=0, keepdims=True)
        kw = kh.astype(F32) * jnp.exp(log_w - a)
        kv_loc = _dot_tn(kw.astype(BF16), vh)
        n_loc = jnp.sum(kw, axis=0, keepdims=True)

        m_new = jnp.maximum(gtot + m_prev, a)
        s_old = jnp.exp(gtot + m_prev - m_new)
        s_new = jnp.exp(a - m_new)
        c_ref[h] = s_old * c_prev + s_new * kv_loc
        n_ref[h] = jnp.broadcast_to(s_old * n_prev + s_new * n_loc, (8, d))
        m_ref[h] = jnp.broadcast_to(m_new, (8, LANES))

        log_d = jnp.where(causal, b_c - b_r + i_r, -jnp.inf)
        log_inter = b_c + m_prev
        m_t = jnp.maximum(log_inter, jnp.max(log_d, axis=1, keepdims=True))
        d_intra = jnp.exp(log_d - m_t)
        s_inter = jnp.exp(log_inter - m_t)
        qk = _dot_nt(qh, kh) * d_intra
        num = _dot(qk.astype(BF16), vh) + s_inter * _dot(qh, c_prev.astype(BF16))
        den = (jnp.sum(qk, axis=1, keepdims=True)
               + s_inter * jnp.sum(qh.astype(F32) * n_prev, axis=1, keepdims=True))
        hh = num / jnp.maximum(jnp.abs(den), jnp.exp(-m_t))

        mu = jnp.mean(hh, axis=1, keepdims=True)
        hc = hh - mu
        var = jnp.mean(hc * hc, axis=1, keepdims=True)
        hn = hc * lax.rsqrt(var + LN_EPS) * hg_ref[:, hs]
        out_ref[0, :, hs] = (jax.nn.sigmoid(o_ref[0, :, hs].astype(F32)) * hn).astype(BF16)


def _mlstm_branch(q, k, v, o, gates, head_g):
    bsz, s, _ = q.shape
    blk = pl.BlockSpec((1, CHUNK, HALF), lambda bi, c: (bi, c, 0))
    return pl.pallas_call(
        _mlstm_kernel,
        grid=(bsz, s // CHUNK),
        in_specs=[blk, blk, blk, blk, pl.BlockSpec((1, CHUNK, LANES), lambda bi, c: (bi, c, 0)),
                  _full_spec(head_g.shape)],
        out_specs=blk,
        out_shape=jax.ShapeDtypeStruct((bsz, s, HALF), BF16),
        scratch_shapes=[pltpu.VMEM((MLSTM_HEADS, MLSTM_HEAD_DIM, MLSTM_HEAD_DIM), F32),
                        pltpu.VMEM((MLSTM_HEADS, 8, MLSTM_HEAD_DIM), F32),
                        pltpu.VMEM((MLSTM_HEADS, 8, LANES), F32)],
        compiler_params=_params("parallel", "arbitrary"),
        name="mlstm_branch",
    )(q, k, v, o, gates, head_g)


def _out_ln_kernel(a_ref, b_ref, x_ref, w_ref, g_ref, beta_ref, out_ref):
    mix = _dot(a_ref[...], w_ref[0:HALF, :]) + _dot(b_ref[...], w_ref[HALF:, :])
    out_ref[...] = _ln(ALPHA * x_ref[...] + mix, g_ref[...], beta_ref[...])


def _out_ln(a, b, x, w, g, beta, tm):
    t = x.shape[0]
    return pl.pallas_call(
        _out_ln_kernel,
        grid=(t // tm,),
        in_specs=[_row_spec(tm, HALF), _row_spec(tm, HALF), _row_spec(tm, D_MODEL),
                  _full_spec(w.shape), _full_spec(g.shape), _full_spec(beta.shape)],
        out_specs=_row_spec(tm, D_MODEL),
        out_shape=jax.ShapeDtypeStruct((t, D_MODEL), F32),
        compiler_params=_params("parallel"),
        name="out_proj_ln",
    )(a, b, x, w, g, beta)


def _swiglu_into(xb_ref, w1_ref, w3_ref, w2_ref, acc_ref):
    acc_ref[...] = jnp.zeros_like(acc_ref)

    def body(c, carry):
        xb = xb_ref[...]
        h1 = _dot(xb, w1_ref[c])
        h3 = _dot(xb, w3_ref[c])
        hh = (h1 * jax.nn.sigmoid(h1) * h3).astype(BF16)
        acc_ref[...] += _dot(hh, w2_ref[c])
        return carry

    lax.fori_loop(0, N_FF_CHUNKS, body, 0)


def _ffn_kernel(x_ref, w1_ref, w3_ref, w2_ref, g_ref, beta_ref, out_ref, xb_ref, acc_ref):
    xb_ref[...] = x_ref[...].astype(BF16)
    _swiglu_into(xb_ref, w1_ref, w3_ref, w2_ref, acc_ref)
    out_ref[...] = _ln(ALPHA * x_ref[...] + acc_ref[...], g_ref[...], beta_ref[...])


def _ffn(x, w1, w3, w2, g, beta, tm):
    t = x.shape[0]
    return pl.pallas_call(
        _ffn_kernel,
        grid=(t // tm,),
        in_specs=[_row_spec(tm, D_MODEL), _full_spec(w1.shape), _full_spec(w3.shape), _full_spec(w2.shape),
                  _full_spec(g.shape), _full_spec(beta.shape)],
        out_specs=_row_spec(tm, D_MODEL),
        out_shape=jax.ShapeDtypeStruct((t, D_MODEL), F32),
        scratch_shapes=[pltpu.VMEM((tm, D_MODEL), BF16), pltpu.VMEM((tm, D_MODEL), F32)],
        compiler_params=_params("parallel"),
        name="dense_swiglu_ln",
    )(x, w1, w3, w2, g, beta)


def _odd_in_kernel(x_ref, w_ref, vg_ref, vb_ref, ws_ref, bs_ref, c_ref, q_ref, k_ref, v_ref):
    tm = x_ref.shape[0]
    xb = x_ref[...].astype(BF16)

    def proj(c):
        return _dot(xb, w_ref[:, c * HALF:(c + 1) * HALF])

    u = _gelu(proj(0))
    zn = _ln(_gelu(proj(1)), vg_ref[...], vb_ref[...]).astype(BF16)
    q_ref[...] = (proj(2) * SB_HEAD_DIM ** -0.5).astype(BF16)
    k_ref[...] = proj(3).astype(BF16)
    v_ref[...] = proj(4).astype(BF16)

    row = lax.broadcasted_iota(jnp.int32, (CHUNK, CHUNK), 0)
    col = lax.broadcasted_iota(jnp.int32, (CHUNK, CHUNK), 1)
    for g in range(GMLP_GROUPS):
        lanes = slice(g * LANES, (g + 1) * LANES)
        wc = jnp.where(col <= row, ws_ref[g], 0.0).astype(BF16)
        bias = bs_ref[:, g:g + 1]
        for r in range(tm // CHUNK):
            rows = slice(r * CHUNK, (r + 1) * CHUNK)
            sg = _dot(wc, zn[rows, lanes]) + bias
            c_ref[rows, lanes] = (u[rows, lanes] * sg).astype(BF16)


def _odd_in(x, w, vg, vb, ws, bs_t, tm):
    t = x.shape[0]
    half_bf = jax.ShapeDtypeStruct((t, HALF), BF16)
    return pl.pallas_call(
        _odd_in_kernel,
        grid=(t // tm,),
        in_specs=[_row_spec(tm, D_MODEL), _full_spec(w.shape), _full_spec(vg.shape), _full_spec(vb.shape),
                  _full_spec(ws.shape), _full_spec(bs_t.shape)],
        out_specs=[_row_spec(tm, HALF)] * 4,
        out_shape=[half_bf] * 4,
        compiler_params=_params("parallel"),
        name="odd_in_proj_gmlp",
    )(x, w, vg, vb, ws, bs_t)


def _sb_kernel(q_ref, k_ref, v_ref, out_ref, lsz_ref, sp_ref, run_ref, acc_ref):
    tq = q_ref.shape[1]
    tk = SB_KEY_TILE
    ratio = tq // tk
    i = pl.program_id(2)
    q = q_ref[0]
    lane = lax.broadcasted_iota(jnp.int32, (tq, LANES), 1)
    zero = jnp.zeros_like(q)
    q_heads = (jnp.where(lane < SB_HEAD_DIM, q, zero), jnp.where(lane >= SB_HEAD_DIM, q, zero))
    row = lax.broadcasted_iota(jnp.int32, (tk, tk), 0)
    col = lax.broadcasted_iota(jnp.int32, (tk, tk), 1)
    later = jnp.where(row > col, 1.0, 0.0).astype(BF16)
    qrow = lax.broadcasted_iota(jnp.int32, (tq, tk), 0)
    kcol = lax.broadcasted_iota(jnp.int32, (tq, tk), 1)
    last = ratio * (i + 1) - 1

    def scores(k, slot, diagonal):
        j = last - k
        kb = k_ref[0, pl.ds(pl.multiple_of(j * tk, tk), tk), :]
        for h, qm in enumerate(q_heads):
            z = _dot_nt(qm, kb)
            sp = jnp.maximum(z, 0.0) + jnp.log(1.0 + jnp.exp2(jnp.abs(z) * -LOG2E))
            lsz = z - sp
            if diagonal:
                visible = kcol + (ratio - 1 - k) * tk < qrow
                sp = jnp.where(visible, sp, 0.0)
                lsz = jnp.where(visible, lsz, MASKED_SCORE)
            sp_ref[slot, h] = sp.astype(BF16)
            run = run_ref[h]
            for lt in range(tk // LANES):
                lanes = slice(lt * LANES, (lt + 1) * LANES)
                lsz_ref[slot, h, :, lanes] = lsz[:, lanes] - run
            run_ref[h] = run + jnp.sum(sp, axis=1, keepdims=True)

    def output(k, slot):
        j = last - k
        vb = v_ref[0, pl.ds(pl.multiple_of(j * tk, tk), tk), :]
        for h in range(len(q_heads)):
            att = jnp.exp2((lsz_ref[slot, h] - _dot(sp_ref[slot, h], later)) * LOG2E)
            acc_ref[h] += _dot(att.astype(BF16), vb)

    run_ref[...] = jnp.zeros_like(run_ref)
    acc_ref[...] = jnp.zeros_like(acc_ref)
    scores(0, 0, True)
    for k in range(1, ratio):
        output(k - 1, (k - 1) % 2)
        scores(k, k % 2, True)

    def two_blocks(p, carry):
        output(2 * p - 1, 1)
        scores(2 * p, 0, False)
        output(2 * p, 0)
        scores(2 * p + 1, 1, False)
        return carry

    lax.fori_loop(ratio // 2, ratio * (i + 1) // 2, two_blocks, 0)
    output(last, 1)
    out_ref[0] = jnp.where(lane < SB_HEAD_DIM, acc_ref[0], acc_ref[1]).astype(BF16)


def _sb_attention(q, k, v, tq):
    bsz, s, _ = q.shape
    kv_spec = pl.BlockSpec((1, s, LANES), lambda bi, p, i: (bi, 0, p))
    q_spec = pl.BlockSpec((1, tq, LANES), lambda bi, p, i: (bi, i, p))
    heads = LANES // SB_HEAD_DIM
    assert tq % (2 * SB_KEY_TILE) == 0, "the walk pairs key blocks, two scratch slots"
    return pl.pallas_call(
        _sb_kernel,
        grid=(bsz, HALF // LANES, s // tq),
        in_specs=[q_spec, kv_spec, kv_spec],
        out_specs=q_spec,
        out_shape=jax.ShapeDtypeStruct((bsz, s, HALF), BF16),
        scratch_shapes=[pltpu.VMEM((2, heads, tq, SB_KEY_TILE), F32), pltpu.VMEM((2, heads, tq, SB_KEY_TILE), BF16),
                        pltpu.VMEM((heads, tq, LANES), F32), pltpu.VMEM((heads, tq, LANES), F32)],
        compiler_params=_params("parallel", "parallel", "arbitrary"),
        name="stick_breaking_attention",
    )(q, k, v)


def _odd_out_kernel(c_ref, d_ref, x_ref, w_ref, g_ref, beta_ref, rw_ref, rb_ref,
                    x1_ref, gate_ref, route_ref, cnt_ref, base_ref):
    tm = x_ref.shape[0]

    @pl.when(pl.program_id(0) == 0)
    def _():
        base_ref[...] = jnp.zeros_like(base_ref)

    mix = _dot(c_ref[...], w_ref[0:HALF, :]) + _dot(d_ref[...], w_ref[HALF:, :])
    x1 = _ln(ALPHA * x_ref[...] + mix, g_ref[...], beta_ref[...])
    x1_ref[...] = x1

    xh, xl = _split_bf16(x1)
    logits = _dot(xh, rw_ref[0]) + _dot(xl, rw_ref[0]) + _dot(xh, rw_ref[1]) + rb_ref[...]
    lane = lax.broadcasted_iota(jnp.int32, (tm, LANES), 1)
    logits = jnp.where(lane < N_EXPERTS, logits, -jnp.inf)
    m1 = jnp.max(logits, axis=1, keepdims=True)
    i1 = jnp.min(jnp.where(logits == m1, lane, LANES), axis=1, keepdims=True)
    sel1 = lane == i1
    rest = jnp.where(sel1, -jnp.inf, logits)
    m2 = jnp.max(rest, axis=1, keepdims=True)
    i2 = jnp.min(jnp.where(rest == m2, lane, LANES), axis=1, keepdims=True)
    sel2 = lane == i2
    e = jnp.exp(m2 - m1)
    g1 = 1.0 / (1.0 + e)
    g2 = e / (1.0 + e)
    first_lower = i1 < i2
    gate_ref[...] = jnp.where(lane == 0, jnp.where(first_lower, g1, g2),
                              jnp.where(lane == 1, jnp.where(first_lower, g2, g1), 0.0))

    sel_f = jnp.where(sel1 | sel2, 1.0, 0.0)
    row = lax.broadcasted_iota(jnp.int32, (tm, tm), 0)
    col = lax.broadcasted_iota(jnp.int32, (tm, tm), 1)
    before = jnp.where(col < row, 1.0, 0.0).astype(BF16)
    rank = base_ref[0:1, :] + _dot(before, sel_f.astype(BF16))
    e_lo = jnp.minimum(i1, i2)
    e_hi = jnp.maximum(i1, i2)
    rank_lo = jnp.sum(jnp.where(lane == e_lo, rank, 0.0), axis=1, keepdims=True)
    rank_hi = jnp.sum(jnp.where(lane == e_hi, rank, 0.0), axis=1, keepdims=True)
    route_ref[...] = jnp.where(lane == 0, e_lo, jnp.where(lane == 1, e_hi, jnp.where(
        lane == 2, rank_lo.astype(jnp.int32), jnp.where(lane == 3, rank_hi.astype(jnp.int32), 0))))
    base_ref[...] = base_ref[...] + jnp.sum(sel_f, axis=0, keepdims=True)
    cnt_ref[...] = base_ref[...].astype(jnp.int32)


def _odd_out(c, d, x, w, g, beta, rw, rb, tm):
    t = x.shape[0]
    lane_spec = _row_spec(tm, LANES)
    return pl.pallas_call(
        _odd_out_kernel,
        grid=(t // tm,),
        in_specs=[_row_spec(tm, HALF), _row_spec(tm, HALF), _row_spec(tm, D_MODEL),
                  _full_spec(w.shape), _full_spec(g.shape), _full_spec(beta.shape),
                  _full_spec(rw.shape), _full_spec(rb.shape)],
        out_specs=[_row_spec(tm, D_MODEL), lane_spec, lane_spec, _full_spec((8, LANES))],
        out_shape=[jax.ShapeDtypeStruct((t, D_MODEL), F32), jax.ShapeDtypeStruct((t, LANES), F32),
                   jax.ShapeDtypeStruct((t, LANES), jnp.int32), jax.ShapeDtypeStruct((8, LANES), jnp.int32)],
        scratch_shapes=[pltpu.VMEM((8, LANES), F32)],
        compiler_params=_params("arbitrary"),
        name="odd_out_proj_ln_router",
    )(c, d, x, w, g, beta, rw, rb)


def _for_rows(n, fn):
    def body(r, carry):
        fn(r)
        return carry
    lax.fori_loop(0, n, body, 0, unroll=8)


def _dispatch_kernel(pos_ref, x_ref, xs_hbm, sem):
    tm = x_ref.shape[0]

    def start(r):
        for k in range(2):
            pltpu.make_async_copy(x_ref.at[pl.ds(r, 1)], xs_hbm.at[pl.ds(pos_ref[0, 0, k * tm + r], 1)], sem).start()

    _for_rows(tm, start)
    for _ in range(2):
        pltpu.make_async_copy(x_ref, xs_hbm.at[pl.ds(0, tm)], sem).wait()


def _dispatch(x1, pos, tm):
    t = x1.shape[0]
    return pl.pallas_call(
        _dispatch_kernel,
        grid=(t // tm,),
        in_specs=[pl.BlockSpec((1, 1, 2 * tm), lambda i: (i, 0, 0), memory_space=pltpu.SMEM),
                  _row_spec(tm, D_MODEL)],
        out_specs=pl.BlockSpec(memory_space=pl.ANY),
        out_shape=jax.ShapeDtypeStruct((2 * t, D_MODEL), F32),
        scratch_shapes=[pltpu.SemaphoreType.DMA(())],
        compiler_params=_params("arbitrary"),
        name="moe_dispatch",
    )(pos, x1)


def _moe_kernel(tile_ref, expert_ref, lo_ref, hi_ref, xs_ref, w1_ref, w3_ref, w2_ref, y_ref, xb_ref, acc_ref):
    step = pl.program_id(0)
    lo = lo_ref[step]
    hi = hi_ref[step]

    @pl.when(hi > lo)
    def _():
        xb_ref[...] = xs_ref[...].astype(BF16)
        _swiglu_into(xb_ref, w1_ref.at[0], w3_ref.at[0], w2_ref.at[0], acc_ref)

        @pl.when(lo == 0)
        def _():
            y_ref[...] = acc_ref[...]

        @pl.when(lo > 0)
        def _():
            row = lax.broadcasted_iota(jnp.int32, y_ref.shape, 0)
            y_ref[...] = jnp.where(row >= lo, acc_ref[...], y_ref[...])


def _moe(xs, step_tile, step_expert, step_lo, step_hi, w1, w3, w2, tm):
    n_steps = step_tile.shape[0]
    w_spec = lambda shape: pl.BlockSpec((1,) + shape[1:], lambda s, tile, ex, lo, hi: (ex[s], 0, 0, 0))
    row_spec = pl.BlockSpec((tm, D_MODEL), lambda s, tile, ex, lo, hi: (tile[s], 0))
    grid_spec = pltpu.PrefetchScalarGridSpec(
        num_scalar_prefetch=4,
        grid=(n_steps,),
        in_specs=[row_spec, w_spec(w1.shape), w_spec(w3.shape), w_spec(w2.shape)],
        out_specs=row_spec,
        scratch_shapes=[pltpu.VMEM((tm, D_MODEL), BF16), pltpu.VMEM((tm, D_MODEL), F32)],
    )
    return pl.pallas_call(
        _moe_kernel,
        grid_spec=grid_spec,
        out_shape=jax.ShapeDtypeStruct(xs.shape, F32),
        compiler_params=_params("arbitrary"),
        name="moe_grouped_swiglu",
    )(step_tile, step_expert, step_lo, step_hi, xs, w1, w3, w2)


def _combine_kernel(pos_ref, pos_next_ref, x_ref, gate_ref, g_ref, beta_ref, y_hbm, out_ref, ybuf, sem):
    tm = x_ref.shape[0]
    i = pl.program_id(0)
    slot = i % 2

    def start_tile(p_ref, buf):
        def start(r):
            for k in range(2):
                pltpu.make_async_copy(y_hbm.at[pl.ds(p_ref[0, 0, k * tm + r], 1)],
                                      ybuf.at[buf, k, pl.ds(r, 1)], sem.at[buf]).start()
        _for_rows(tm, start)

    @pl.when(i == 0)
    def _():
        start_tile(pos_ref, 0)

    @pl.when(i + 1 < pl.num_programs(0))
    def _():
        start_tile(pos_next_ref, 1 - slot)

    for k in range(2):
        pltpu.make_async_copy(y_hbm.at[pl.ds(0, tm)], ybuf.at[slot, k], sem.at[slot]).wait()
    gate = gate_ref[...]
    ffn = gate[:, 0:1] * ybuf[slot, 0] + gate[:, 1:2] * ybuf[slot, 1]
    out_ref[...] = _ln(ALPHA * x_ref[...] + ffn, g_ref[...], beta_ref[...])


def _combine(x1, y, pos, gate, g, beta, tm):
    t = x1.shape[0]
    nt = t // tm
    pos_spec = lambda f: pl.BlockSpec((1, 1, 2 * tm), f, memory_space=pltpu.SMEM)
    return pl.pallas_call(
        _combine_kernel,
        grid=(nt,),
        in_specs=[pos_spec(lambda i: (i, 0, 0)), pos_spec(lambda i: (jnp.minimum(i + 1, nt - 1), 0, 0)),
                  _row_spec(tm, D_MODEL), _row_spec(tm, LANES), _full_spec(g.shape), _full_spec(beta.shape),
                  pl.BlockSpec(memory_space=pl.ANY)],
        out_specs=_row_spec(tm, D_MODEL),
        out_shape=jax.ShapeDtypeStruct((t, D_MODEL), F32),
        scratch_shapes=[pltpu.VMEM((2, 2, tm, D_MODEL), F32), pltpu.SemaphoreType.DMA((2,))],
        compiler_params=_params("arbitrary"),
        name="moe_combine_ln",
    )(pos, pos, x1, gate, g, beta, y)


def _routing_tables(route, cnt, t, tm):
    counts = cnt[0, :N_EXPERTS]
    ends = jnp.cumsum(counts)
    starts = ends - counts
    experts = jnp.arange(N_EXPERTS, dtype=jnp.int32)

    def sorted_row(e, rank):
        return jnp.sum(jnp.where(e[:, None] == experts[None, :], starts[None, :], 0), axis=1) + rank

    pos = jnp.stack([sorted_row(route[:, 0], route[:, 2]), sorted_row(route[:, 1], route[:, 3])])
    nt = t // tm
    pos = pos.reshape(2, nt, tm).transpose(1, 0, 2).reshape(nt, 1, 2 * tm)

    n_steps = (2 * t) // tm + N_EXPERTS - 1
    first_tile = starts // tm
    n_visits = jnp.where(counts > 0, (ends - 1) // tm - first_tile + 1, 0)
    step_ends = jnp.cumsum(n_visits)
    total = step_ends[-1]
    s = jnp.minimum(jnp.arange(n_steps, dtype=jnp.int32), total - 1)
    step_expert = jnp.sum((s[:, None] >= step_ends[None, :]).astype(jnp.int32), axis=1)
    step_tile = first_tile[step_expert] + s - (step_ends - n_visits)[step_expert]
    lo = jnp.clip(starts[step_expert] - step_tile * tm, 0, tm)
    hi = jnp.clip(ends[step_expert] - step_tile * tm, 0, tm)
    hi = jnp.where(jnp.arange(n_steps) < total, hi, lo)
    as_i32 = lambda a: a.astype(jnp.int32)
    return pos.astype(jnp.int32), as_i32(step_tile), as_i32(step_expert), as_i32(lo), as_i32(hi)


def _row(v):
    return v.reshape(1, -1).astype(F32)


def _ff_chunks_in(w):
    lead = w.shape[:-2]
    w = w.astype(BF16).reshape(lead + (D_MODEL, N_FF_CHUNKS, FF_CHUNK))
    return jnp.swapaxes(w, -3, -2)


def _ff_chunks_out(w):
    return w.astype(BF16).reshape(w.shape[:-2] + (N_FF_CHUNKS, FF_CHUNK, D_MODEL))


def _even_layer(x, bsz, s, w_in, gate_bias, conv_w, conv_b, norm_g, norm_b, head_g, w_out,
                ln1_g, ln1_b, w1, w3, w2, ln2_g, ln2_b):
    main = 6 * HALF
    w_main = w_in[:, :main].astype(BF16)
    w_gate = jnp.pad(w_in[:, main:], ((0, 0), (0, LANES - 2 * MLSTM_HEADS))).astype(BF16)
    gb = jnp.pad(gate_bias, (0, LANES - 2 * MLSTM_HEADS)).reshape(1, LANES)
    glu, q, k, v, o, gates = _even_in(x, w_main, w_gate, gb, TOKEN_TILE)
    shp = lambda a: a.reshape(bsz, s, a.shape[-1])
    conv_w32 = jnp.pad(conv_w, ((0, 32 - CONV_WIDTH), (0, 0)))
    a = _conv_branch(shp(glu), conv_w32, _row(conv_b), _row(norm_g), _row(norm_b), min(s, 512))
    hb = _mlstm_branch(shp(q), shp(k), shp(v), shp(o), shp(gates), _row(head_g))
    x1 = _out_ln(a.reshape(-1, HALF), hb.reshape(-1, HALF), x, w_out.astype(BF16), _row(ln1_g), _row(ln1_b),
                 TOKEN_TILE)
    return _ffn(x1, _ff_chunks_in(w1), _ff_chunks_in(w3), _ff_chunks_out(w2), _row(ln2_g), _row(ln2_b), TOKEN_TILE)


def _odd_layer(x, bsz, s, w_in, v_g, v_b, w_s, b_s, w_out, ln1_g, ln1_b, router_w, router_b,
               w1, w3, w2, ln2_g, ln2_b):
    t = x.shape[0]
    c, q, k, v = _odd_in(x, w_in.astype(BF16), _row(v_g), _row(v_b), w_s, b_s.T, TOKEN_TILE)
    shp = lambda a: a.reshape(bsz, s, HALF)
    d = _sb_attention(shp(q), shp(k), shp(v), min(s, SB_TILE)).reshape(t, HALF)
    rw = jnp.pad(router_w, ((0, 0), (0, LANES - N_EXPERTS)))
    rw_hi = rw.astype(BF16)
    rw_lo = (rw - rw_hi.astype(F32)).astype(BF16)
    rb = jnp.pad(router_b, (0, LANES - N_EXPERTS)).reshape(1, LANES)
    x1, gate, route, cnt = _odd_out(c, d, x, w_out.astype(BF16), _row(ln1_g), _row(ln1_b),
                                    jnp.stack([rw_hi, rw_lo]), rb, TOKEN_TILE)
    pos, step_tile, step_expert, step_lo, step_hi = _routing_tables(route, cnt, t, MOE_TILE)
    xs = _dispatch(x1, pos, MOE_TILE)
    y = _moe(xs, step_tile, step_expert, step_lo, step_hi,
             _ff_chunks_in(w1), _ff_chunks_in(w3), _ff_chunks_out(w2), MOE_TILE)
    return _combine(x1, y, pos, gate, _row(ln2_g), _row(ln2_b), MOE_TILE)


def kernel(x, ab_w_in, ab_gate_bias, a_conv_w, a_conv_b, a_norm_g, a_norm_b, b_norm_g, ab_w_out, ab_ln1_g, ab_ln1_b, ffn_w1, ffn_w3, ffn_w2, ab_ln2_g, ab_ln2_b, cd_w_in, c_norm_g, c_norm_b, c_w_s, c_b_s, cd_w_out, cd_ln1_g, cd_ln1_b, router_w, router_b, moe_w1, moe_w3, moe_w2, cd_ln2_g, cd_ln2_b):
    bsz, s, _ = x.shape
    h = x.reshape(bsz * s, D_MODEL)
    for layer in range(DEPTH):
        j = layer // 2
        if layer % 2 == 0:
            h = _even_layer(h, bsz, s, ab_w_in[j], ab_gate_bias[j], a_conv_w[j], a_conv_b[j], a_norm_g[j],
                            a_norm_b[j], b_norm_g[j], ab_w_out[j], ab_ln1_g[j], ab_ln1_b[j],
                            ffn_w1[j], ffn_w3[j], ffn_w2[j], ab_ln2_g[j], ab_ln2_b[j])
        else:
            h = _odd_layer(h, bsz, s, cd_w_in[j], c_norm_g[j], c_norm_b[j], c_w_s[j], c_b_s[j], cd_w_out[j],
                           cd_ln1_g[j], cd_ln1_b[j], router_w[j], router_b[j],
                           moe_w1[j], moe_w3[j], moe_w2[j], cd_ln2_g[j], cd_ln2_b[j])
    return h.reshape(bsz, s, D_MODEL)
```

```python
import jax
import jax.numpy as jnp
from jax import lax
from jax.experimental import pallas as pl
from jax.experimental.pallas import tpu as pltpu

F32 = jnp.float32
BF16 = jnp.bfloat16

D_MODEL = 1024
DEPTH = 4
HALF = D_MODEL // 2
CONV_WIDTH = 31
CONV_ROWS = 16
CONV_HALO = 32
MLSTM_HEADS = 4
MLSTM_HEAD_DIM = HALF // MLSTM_HEADS
CHUNK = 128
GMLP_GROUPS = 4
SB_HEADS = 8
SB_HEAD_DIM = HALF // SB_HEADS
D_FF = 2816
FF_CHUNK = 512
N_EXPERTS = 8
ALPHA = (2 * DEPTH) ** 0.25
LN_EPS = 1e-5
LOG2E = 1.4426950408889634
MASKED_SCORE = -1e30
LANES = 128
SLAB = D_MODEL // LANES
V7X_VMEM_LIMIT = 56 * 1024 * 1024

TOKEN_TILE = 512
MOE_TILE = 512
SB_TILE = 512
SB_BATCH = 2
SB_KEY_TILE = 256


def _ln(x, g, b):
    mu = jnp.mean(x, axis=-1, keepdims=True)
    xc = x - mu
    var = jnp.mean(xc * xc, axis=-1, keepdims=True)
    return xc * lax.rsqrt(var + LN_EPS) * g + b


def _gelu(x):
    return 0.5 * x * (1.0 + lax.erf(x * (0.5 ** 0.5)))


def _dot(a, b):
    return jnp.dot(a, b, preferred_element_type=F32)


def _dot_nt(a, b):
    return lax.dot_general(a, b, (((1,), (1,)), ((), ())), preferred_element_type=F32)


def _dot_tn(a, b):
    return lax.dot_general(a, b, (((0,), (0,)), ((), ())), preferred_element_type=F32)


def _split_bf16(x):
    hi = x.astype(BF16)
    lo = (x - hi.astype(F32)).astype(BF16)
    return hi, lo


def _params(*sem):
    return pltpu.CompilerParams(dimension_semantics=sem, vmem_limit_bytes=V7X_VMEM_LIMIT)


def _row_spec(tm, width):
    return pl.BlockSpec((tm, width), lambda i: (i, 0))


def _full_spec(shape):
    zeros = (0,) * len(shape)
    return pl.BlockSpec(shape, lambda *_: zeros)


def _even_in_kernel(x_ref, w_ref, wg_ref, gb_ref, glu_ref, q_ref, k_ref, v_ref, o_ref, g_ref):
    xb = x_ref[...].astype(BF16)

    def proj(c):
        return _dot(xb, w_ref[:, c * HALF:(c + 1) * HALF])

    glu_ref[...] = proj(0) * jax.nn.sigmoid(proj(1))
    q_ref[...] = proj(2).astype(BF16)
    k_ref[...] = (proj(3) * MLSTM_HEAD_DIM ** -0.5).astype(BF16)
    v_ref[...] = proj(4).astype(BF16)
    o_ref[...] = proj(5).astype(BF16)
    g_ref[...] = _dot(xb, wg_ref[...]) + gb_ref[...]


def _even_in(x, w, wg, gb, tm):
    t = x.shape[0]
    half_bf = jax.ShapeDtypeStruct((t, HALF), BF16)
    return pl.pallas_call(
        _even_in_kernel,
        grid=(t // tm,),
        in_specs=[_row_spec(tm, D_MODEL), _full_spec(w.shape), _full_spec(wg.shape), _full_spec(gb.shape)],
        out_specs=[_row_spec(tm, HALF)] * 5 + [_row_spec(tm, LANES)],
        out_shape=[jax.ShapeDtypeStruct((t, HALF), F32), half_bf, half_bf, half_bf, half_bf,
                   jax.ShapeDtypeStruct((t, LANES), F32)],
        compiler_params=_params("parallel"),
        name="even_in_proj",
    )(x, w, wg, gb)


def _conv_kernel(prev_ref, cur_ref, w_ref, cb_ref, g_ref, b_ref, out_ref, win_ref, y_ref):
    nseq = cur_ref.shape[0]
    ts = cur_ref.shape[1]
    nblk = HALF // LANES
    first = pl.program_id(1) == 0
    for b in range(nseq):
        for c in range(nblk):
            lanes = slice(c * LANES, (c + 1) * LANES)
            sub = b * nblk + c
            win_ref[pl.ds(sub, CONV_HALO, stride=SLAB), :] = jnp.where(first, 0.0, prev_ref[b, :, lanes])
            win_ref[pl.ds(CONV_HALO * SLAB + sub, ts, stride=SLAB), :] = cur_ref[b, :, lanes]
    lead = CONV_HALO - (CONV_WIDTH - 1)
    rows = CONV_ROWS * SLAB

    def body(n, carry):
        base = pl.multiple_of(n * rows, rows)
        acc = jnp.tile(cb_ref[...], (CONV_ROWS, 1))
        for j in range(CONV_WIDTH):
            acc = acc + jnp.tile(w_ref[j], (CONV_ROWS, 1)) * win_ref[pl.ds(base + (lead + j) * SLAB, rows), :]
        y_ref[pl.ds(base, rows), :] = acc
        return carry

    lax.fori_loop(0, ts // CONV_ROWS, body, 0)
    for b in range(nseq):
        y = jnp.concatenate([y_ref[pl.ds(b * nblk + c, ts, stride=SLAB), :] for c in range(nblk)], axis=1)
        yn = _ln(y, g_ref[...], b_ref[...])
        out_ref[b] = (yn * jax.nn.sigmoid(yn)).astype(BF16)


def _conv_branch(glu, w, cb, g, b, ts):
    bsz, s, _ = glu.shape
    nseq = SLAB * LANES // HALF
    assert bsz % nseq == 0
    per = ts // CONV_HALO
    w_tiles = jnp.tile(jnp.pad(w, ((0, 32 - CONV_WIDTH), (0, 0))).reshape(32, HALF // LANES, LANES), (1, nseq, 1))
    cb_tile = jnp.tile(cb.reshape(HALF // LANES, LANES), (nseq, 1))
    return pl.pallas_call(
        _conv_kernel,
        grid=(bsz // nseq, s // ts),
        in_specs=[
            pl.BlockSpec((nseq, CONV_HALO, HALF), lambda bi, i: (bi, jnp.maximum(i * per - 1, 0), 0)),
            pl.BlockSpec((nseq, ts, HALF), lambda bi, i: (bi, i, 0)),
            _full_spec(w_tiles.shape), _full_spec(cb_tile.shape), _full_spec(g.shape), _full_spec(b.shape),
        ],
        out_specs=pl.BlockSpec((nseq, ts, HALF), lambda bi, i: (bi, i, 0)),
        out_shape=jax.ShapeDtypeStruct((bsz, s, HALF), BF16),
        scratch_shapes=[pltpu.VMEM(((ts + CONV_HALO) * SLAB, LANES), F32), pltpu.VMEM((ts * SLAB, LANES), F32)],
        compiler_params=_params("parallel", "parallel"),
        name="conv_branch",
    )(glu, glu, w_tiles, cb_tile, g, b)


def _mlstm_kernel(q_ref, k_ref, v_ref, o_ref, g_ref, hg_ref, out_ref, c_ref, n_ref, m_ref):
    L, H, d = CHUNK, MLSTM_HEADS, MLSTM_HEAD_DIM

    @pl.when(pl.program_id(1) == 0)
    def _():
        c_ref[...] = jnp.zeros_like(c_ref)
        n_ref[...] = jnp.zeros_like(n_ref)
        m_ref[...] = jnp.zeros_like(m_ref)

    gates = g_ref[0]
    logf = jax.nn.log_sigmoid(gates)
    row = lax.broadcasted_iota(jnp.int32, (L, L), 0)
    col = lax.broadcasted_iota(jnp.int32, (L, L), 1)
    causal = col <= row
    tri = jnp.where(causal, 1.0, 0.0).astype(BF16)
    lf_hi, lf_lo = _split_bf16(logf)
    bcum = _dot(tri, lf_hi) + _dot(tri, lf_lo)
    gates_t = gates.T
    bcum_t = bcum.T

    for h in range(H):
        hs = slice(h * d, (h + 1) * d)
        qh, kh, vh = q_ref[0, :, hs], k_ref[0, :, hs], v_ref[0, :, hs]
        i_c = gates[:, h:h + 1]
        b_c = bcum[:, H + h:H + h + 1]
        i_r = gates_t[h:h + 1, :]
        b_r = bcum_t[H + h:H + h + 1, :]
        gtot = bcum[L - 1:L, H + h:H + h + 1]

        c_prev = c_ref[h]
        n_prev = n_ref[h, 0:1, :]
        m_prev = m_ref[h, 0:1, 0:1]

        log_w = gtot - b_c + i_c
        a = jnp.max(log_w, axis=0, keepdims=True)
        kw = kh.astype(F32) * jnp.exp(log_w - a)
        kv_loc = _dot_tn(kw.astype(BF16), vh)
        n_loc = jnp.sum(kw, axis=0, keepdims=True)

        m_new = jnp.maximum(gtot + m_prev, a)
        s_old = jnp.exp(gtot + m_prev - m_new)
        s_new = jnp.exp(a - m_new)
        c_ref[h] = s_old * c_prev + s_new * kv_loc
        n_ref[h] = jnp.broadcast_to(s_old * n_prev + s_new * n_loc, (8, d))
        m_ref[h] = jnp.broadcast_to(m_new, (8, LANES))

        log_d = jnp.where(causal, b_c - b_r + i_r, -jnp.inf)
        log_inter = b_c + m_prev
        m_t = jnp.maximum(log_inter, jnp.max(log_d, axis=1, keepdims=True))
        d_intra = jnp.exp(log_d - m_t)
        s_inter = jnp.exp(log_inter - m_t)
        qk = _dot_nt(qh, kh) * d_intra
        num = _dot(qk.astype(BF16), vh) + s_inter * _dot(qh, c_prev.astype(BF16))
        den = jnp.sum(qk + s_inter * (qh.astype(F32) * n_prev), axis=1, keepdims=True)
        hh = num / jnp.maximum(jnp.abs(den), jnp.exp(-m_t))

        mu = jnp.mean(hh, axis=1, keepdims=True)
        hc = hh - mu
        var = jnp.mean(hc * hc, axis=1, keepdims=True)
        hn = hc * lax.rsqrt(var + LN_EPS) * hg_ref[:, hs]
        out_ref[0, :, hs] = (jax.nn.sigmoid(o_ref[0, :, hs].astype(F32)) * hn).astype(BF16)


def _mlstm_branch(q, k, v, o, gates, head_g):
    bsz, s, _ = q.shape
    blk = pl.BlockSpec((1, CHUNK, HALF), lambda bi, c: (bi, c, 0))
    return pl.pallas_call(
        _mlstm_kernel,
        grid=(bsz, s // CHUNK),
        in_specs=[blk, blk, blk, blk, pl.BlockSpec((1, CHUNK, LANES), lambda bi, c: (bi, c, 0)),
                  _full_spec(head_g.shape)],
        out_specs=blk,
        out_shape=jax.ShapeDtypeStruct((bsz, s, HALF), BF16),
        scratch_shapes=[pltpu.VMEM((MLSTM_HEADS, MLSTM_HEAD_DIM, MLSTM_HEAD_DIM), F32),
                        pltpu.VMEM((MLSTM_HEADS, 8, MLSTM_HEAD_DIM), F32),
                        pltpu.VMEM((MLSTM_HEADS, 8, LANES), F32)],
        compiler_params=_params("parallel", "arbitrary"),
        name="mlstm_branch",
    )(q, k, v, o, gates, head_g)


def _out_ln_kernel(a_ref, b_ref, x_ref, w_ref, g_ref, beta_ref, out_ref):
    mix = _dot(a_ref[...], w_ref[0:HALF, :]) + _dot(b_ref[...], w_ref[HALF:, :])
    out_ref[...] = _ln(ALPHA * x_ref[...] + mix, g_ref[...], beta_ref[...])


def _out_ln(a, b, x, w, g, beta, tm):
    t = x.shape[0]
    return pl.pallas_call(
        _out_ln_kernel,
        grid=(t // tm,),
        in_specs=[_row_spec(tm, HALF), _row_spec(tm, HALF), _row_spec(tm, D_MODEL),
                  _full_spec(w.shape), _full_spec(g.shape), _full_spec(beta.shape)],
        out_specs=_row_spec(tm, D_MODEL),
        out_shape=jax.ShapeDtypeStruct((t, D_MODEL), F32),
        compiler_params=_params("parallel"),
        name="out_proj_ln",
    )(a, b, x, w, g, beta)


def _swiglu_into(xb_ref, w1_ref, w3_ref, w2_ref, acc_ref):
    for n, start in enumerate(range(0, D_FF, FF_CHUNK)):
        size = min(FF_CHUNK, D_FF - start)
        xb = xb_ref[...]
        h1 = _dot(xb, w1_ref[:, start:start + size])
        h3 = _dot(xb, w3_ref[:, start:start + size])
        hh = (h1 * jax.nn.sigmoid(h1) * h3).astype(BF16)
        out = _dot(hh, w2_ref[start:start + size, :])
        if n == 0:
            acc_ref[...] = out
        else:
            acc_ref[...] += out


def _ffn_kernel(x_ref, w1_ref, w3_ref, w2_ref, g_ref, beta_ref, out_ref, xb_ref, acc_ref):
    xb_ref[...] = x_ref[...].astype(BF16)
    _swiglu_into(xb_ref, w1_ref, w3_ref, w2_ref, acc_ref)
    out_ref[...] = _ln(ALPHA * x_ref[...] + acc_ref[...], g_ref[...], beta_ref[...])


def _ffn(x, w1, w3, w2, g, beta, tm):
    t = x.shape[0]
    return pl.pallas_call(
        _ffn_kernel,
        grid=(t // tm,),
        in_specs=[_row_spec(tm, D_MODEL), _full_spec(w1.shape), _full_spec(w3.shape), _full_spec(w2.shape),
                  _full_spec(g.shape), _full_spec(beta.shape)],
        out_specs=_row_spec(tm, D_MODEL),
        out_shape=jax.ShapeDtypeStruct((t, D_MODEL), F32),
        scratch_shapes=[pltpu.VMEM((tm, D_MODEL), BF16), pltpu.VMEM((tm, D_MODEL), F32)],
        compiler_params=_params("parallel"),
        name="dense_swiglu_ln",
    )(x, w1, w3, w2, g, beta)


def _odd_in_kernel(x_ref, w_ref, vg_ref, vb_ref, ws_ref, bs_ref, c_ref, q_ref, k_ref, v_ref):
    tm = x_ref.shape[0]
    xb = x_ref[...].astype(BF16)

    def proj(c):
        return _dot(xb, w_ref[:, c * HALF:(c + 1) * HALF])

    u = _gelu(proj(0))
    zn = _ln(_gelu(proj(1)), vg_ref[...], vb_ref[...]).astype(BF16)
    q_ref[...] = (proj(2) * SB_HEAD_DIM ** -0.5).astype(BF16)
    k_ref[...] = proj(3).astype(BF16)
    v_ref[...] = proj(4).astype(BF16)

    row = lax.broadcasted_iota(jnp.int32, (CHUNK, CHUNK), 0)
    col = lax.broadcasted_iota(jnp.int32, (CHUNK, CHUNK), 1)
    for g in range(GMLP_GROUPS):
        lanes = slice(g * LANES, (g + 1) * LANES)
        wc = jnp.where(col <= row, ws_ref[g], 0.0).astype(BF16)
        bias = bs_ref[:, g:g + 1]
        for r in range(tm // CHUNK):
            rows = slice(r * CHUNK, (r + 1) * CHUNK)
            sg = _dot(wc, zn[rows, lanes]) + bias
            c_ref[rows, lanes] = (u[rows, lanes] * sg).astype(BF16)


def _odd_in(x, w, vg, vb, ws, bs_t, tm):
    t = x.shape[0]
    half_bf = jax.ShapeDtypeStruct((t, HALF), BF16)
    return pl.pallas_call(
        _odd_in_kernel,
        grid=(t // tm,),
        in_specs=[_row_spec(tm, D_MODEL), _full_spec(w.shape), _full_spec(vg.shape), _full_spec(vb.shape),
                  _full_spec(ws.shape), _full_spec(bs_t.shape)],
        out_specs=[_row_spec(tm, HALF)] * 4,
        out_shape=[half_bf] * 4,
        compiler_params=_params("parallel"),
        name="odd_in_proj_gmlp",
    )(x, w, vg, vb, ws, bs_t)


def _sb_kernel(q_ref, k_ref, v_ref, out_ref, lsz_ref, sp_ref, run_ref, acc_ref):
    nb = q_ref.shape[0]
    tq = q_ref.shape[1]
    tk = SB_KEY_TILE
    ratio = tq // tk
    i = pl.program_id(2)
    lane = lax.broadcasted_iota(jnp.int32, (tq, LANES), 1)
    row = lax.broadcasted_iota(jnp.int32, (tk, tk), 0)
    col = lax.broadcasted_iota(jnp.int32, (tk, tk), 1)
    later = jnp.where(row > col, 1.0, 0.0).astype(BF16)
    qrow = lax.broadcasted_iota(jnp.int32, (tq, tk), 0)
    kcol = lax.broadcasted_iota(jnp.int32, (tq, tk), 1)
    last = ratio * (i + 1) - 1
    streams = []
    for b in range(nb):
        q = q_ref[b]
        zero = jnp.zeros_like(q)
        streams.append((b, 0, jnp.where(lane < SB_HEAD_DIM, q, zero)))
        streams.append((b, 1, jnp.where(lane >= SB_HEAD_DIM, q, zero)))

    def scores(k, slot, diagonal):
        j = last - k
        for b, h, qm in streams:
            kb = k_ref[b, pl.ds(pl.multiple_of(j * tk, tk), tk), :]
            z = _dot_nt(qm, kb)
            sp = jnp.maximum(z, 0.0) + jnp.log(1.0 + jnp.exp2(jnp.abs(z) * -LOG2E))
            lsz = z - sp
            if diagonal:
                visible = kcol + (ratio - 1 - k) * tk < qrow
                sp = jnp.where(visible, sp, 0.0)
                lsz = jnp.where(visible, lsz, MASKED_SCORE)
            sp_ref[slot, b, h] = sp.astype(BF16)
            run = run_ref[b, h]
            for lt in range(tk // LANES):
                lanes = slice(lt * LANES, (lt + 1) * LANES)
                lsz_ref[slot, b, h, :, lanes] = lsz[:, lanes] - run
            run_ref[b, h] = run + jnp.sum(sp, axis=1, keepdims=True)

    def output(k, slot):
        j = last - k
        for b, h, _ in streams:
            vb = v_ref[b, pl.ds(pl.multiple_of(j * tk, tk), tk), :]
            att = jnp.exp2((lsz_ref[slot, b, h] - _dot(sp_ref[slot, b, h], later)) * LOG2E)
            acc_ref[b, h] += _dot(att.astype(BF16), vb)

    run_ref[...] = jnp.zeros_like(run_ref)
    acc_ref[...] = jnp.zeros_like(acc_ref)
    scores(0, 0, True)
    for k in range(1, ratio):
        output(k - 1, (k - 1) % 2)
        scores(k, k % 2, True)

    def two_blocks(p, carry):
        output(2 * p - 1, 1)
        scores(2 * p, 0, False)
        output(2 * p, 0)
        scores(2 * p + 1, 1, False)
        return carry

    lax.fori_loop(ratio // 2, ratio * (i + 1) // 2, two_blocks, 0)
    output(last, 1)
    for b in range(nb):
        out_ref[b] = jnp.where(lane < SB_HEAD_DIM, acc_ref[b, 0], acc_ref[b, 1]).astype(BF16)


def _sb_attention(q, k, v, tq):
    bsz, s, _ = q.shape
    nb = SB_BATCH if bsz % SB_BATCH == 0 else 1
    kv_spec = pl.BlockSpec((nb, s, LANES), lambda bi, p, i: (bi, 0, p))
    q_spec = pl.BlockSpec((nb, tq, LANES), lambda bi, p, i: (bi, i, p))
    heads = LANES // SB_HEAD_DIM
    assert tq % (2 * SB_KEY_TILE) == 0, "the walk pairs key blocks, two scratch slots"
    return pl.pallas_call(
        _sb_kernel,
        grid=(bsz // nb, HALF // LANES, s // tq),
        in_specs=[q_spec, kv_spec, kv_spec],
        out_specs=q_spec,
        out_shape=jax.ShapeDtypeStruct((bsz, s, HALF), BF16),
        scratch_shapes=[pltpu.VMEM((2, nb, heads, tq, SB_KEY_TILE), F32),
                        pltpu.VMEM((2, nb, heads, tq, SB_KEY_TILE), BF16),
                        pltpu.VMEM((nb, heads, tq, LANES), F32), pltpu.VMEM((nb, heads, tq, LANES), F32)],
        compiler_params=_params("parallel", "parallel", "arbitrary"),
        name="stick_breaking_attention",
    )(q, k, v)


def _odd_out_kernel(c_ref, d_ref, x_ref, w_ref, g_ref, beta_ref, rw_ref, rb_ref,
                    x1_ref, gate_ref, route_ref, cnt_ref, base_ref):
    tm = x_ref.shape[0]

    @pl.when(pl.program_id(0) == 0)
    def _():
        base_ref[...] = jnp.zeros_like(base_ref)

    mix = _dot(c_ref[...], w_ref[0:HALF, :]) + _dot(d_ref[...], w_ref[HALF:, :])
    x1 = _ln(ALPHA * x_ref[...] + mix, g_ref[...], beta_ref[...])
    x1_ref[...] = x1

    xh, xl = _split_bf16(x1)
    logits = _dot(xh, rw_ref[0]) + _dot(xl, rw_ref[0]) + _dot(xh, rw_ref[1]) + rb_ref[...]
    lane = lax.broadcasted_iota(jnp.int32, (tm, LANES), 1)
    logits = jnp.where(lane < N_EXPERTS, logits, -jnp.inf)
    m1 = jnp.max(logits, axis=1, keepdims=True)
    i1 = jnp.min(jnp.where(logits == m1, lane, LANES), axis=1, keepdims=True)
    sel1 = lane == i1
    rest = jnp.where(sel1, -jnp.inf, logits)
    m2 = jnp.max(rest, axis=1, keepdims=True)
    i2 = jnp.min(jnp.where(rest == m2, lane, LANES), axis=1, keepdims=True)
    sel2 = lane == i2
    e = jnp.exp(m2 - m1)
    g1 = 1.0 / (1.0 + e)
    g2 = e / (1.0 + e)
    first_lower = i1 < i2
    gate_ref[...] = jnp.where(lane == 0, jnp.where(first_lower, g1, g2),
                              jnp.where(lane == 1, jnp.where(first_lower, g2, g1), 0.0))

    sel_f = jnp.where(sel1 | sel2, 1.0, 0.0)
    row = lax.broadcasted_iota(jnp.int32, (tm, tm), 0)
    col = lax.broadcasted_iota(jnp.int32, (tm, tm), 1)
    before = jnp.where(col < row, 1.0, 0.0).astype(BF16)
    rank = base_ref[0:1, :] + _dot(before, sel_f.astype(BF16))
    e_lo = jnp.minimum(i1, i2)
    e_hi = jnp.maximum(i1, i2)
    rank_lo = jnp.sum(jnp.where(lane == e_lo, rank, 0.0), axis=1, keepdims=True)
    rank_hi = jnp.sum(jnp.where(lane == e_hi, rank, 0.0), axis=1, keepdims=True)
    route_ref[...] = jnp.where(lane == 0, e_lo, jnp.where(lane == 1, e_hi, jnp.where(
        lane == 2, rank_lo.astype(jnp.int32), jnp.where(lane == 3, rank_hi.astype(jnp.int32), 0))))
    base_ref[...] = base_ref[...] + jnp.sum(sel_f, axis=0, keepdims=True)
    cnt_ref[...] = base_ref[...].astype(jnp.int32)


def _odd_out(c, d, x, w, g, beta, rw, rb, tm):
    t = x.shape[0]
    lane_spec = _row_spec(tm, LANES)
    return pl.pallas_call(
        _odd_out_kernel,
        grid=(t // tm,),
        in_specs=[_row_spec(tm, HALF), _row_spec(tm, HALF), _row_spec(tm, D_MODEL),
                  _full_spec(w.shape), _full_spec(g.shape), _full_spec(beta.shape),
                  _full_spec(rw.shape), _full_spec(rb.shape)],
        out_specs=[_row_spec(tm, D_MODEL), lane_spec, lane_spec, _full_spec((8, LANES))],
        out_shape=[jax.ShapeDtypeStruct((t, D_MODEL), F32), jax.ShapeDtypeStruct((t, LANES), F32),
                   jax.ShapeDtypeStruct((t, LANES), jnp.int32), jax.ShapeDtypeStruct((8, LANES), jnp.int32)],
        scratch_shapes=[pltpu.VMEM((8, LANES), F32)],
        compiler_params=_params("arbitrary"),
        name="odd_out_proj_ln_router",
    )(c, d, x, w, g, beta, rw, rb)


def _for_rows(n, fn):
    def body(r, carry):
        fn(r)
        return carry
    lax.fori_loop(0, n, body, 0, unroll=8)


def _dispatch_kernel(pos_ref, x_ref, xs_hbm, slab_ref, sem):
    tm = x_ref.shape[0]
    for s in range(SLAB):
        slab_ref[pl.ds(s, tm, stride=SLAB), :] = x_ref[:, s * LANES:(s + 1) * LANES]

    def start(r):
        src = slab_ref.at[pl.ds(pl.multiple_of(r * SLAB, SLAB), SLAB)]
        for k in range(2):
            dst = pl.multiple_of(pos_ref[0, 0, k * tm + r] * SLAB, SLAB)
            pltpu.make_async_copy(src, xs_hbm.at[pl.ds(dst, SLAB)], sem).start()

    _for_rows(tm, start)
    for _ in range(2):
        pltpu.make_async_copy(slab_ref, xs_hbm.at[pl.ds(0, tm * SLAB)], sem).wait()


def _dispatch(x1, pos, tm):
    t = x1.shape[0]
    return pl.pallas_call(
        _dispatch_kernel,
        grid=(t // tm,),
        in_specs=[pl.BlockSpec((1, 1, 2 * tm), lambda i: (i, 0, 0), memory_space=pltpu.SMEM),
                  _row_spec(tm, D_MODEL)],
        out_specs=pl.BlockSpec(memory_space=pl.ANY),
        out_shape=jax.ShapeDtypeStruct((2 * t * SLAB, LANES), F32),
        scratch_shapes=[pltpu.VMEM((tm * SLAB, LANES), F32), pltpu.SemaphoreType.DMA(())],
        compiler_params=_params("arbitrary"),
        name="moe_dispatch",
    )(pos, x1)


def _moe_kernel(tile_ref, expert_ref, lo_ref, hi_ref, xs_ref, w1_ref, w3_ref, w2_ref, y_ref, xb_ref, acc_ref):
    tm = xb_ref.shape[0]
    step = pl.program_id(0)
    lo = lo_ref[step]
    hi = hi_ref[step]

    @pl.when(hi > lo)
    def _():
        for s in range(SLAB):
            xb_ref[:, s * LANES:(s + 1) * LANES] = xs_ref[pl.ds(s, tm, stride=SLAB), :].astype(BF16)
        _swiglu_into(xb_ref, w1_ref.at[0], w3_ref.at[0], w2_ref.at[0], acc_ref)

        @pl.when(lo == 0)
        def _():
            for s in range(SLAB):
                y_ref[pl.ds(s, tm, stride=SLAB), :] = acc_ref[:, s * LANES:(s + 1) * LANES]

        @pl.when(lo > 0)
        def _():
            row = lax.broadcasted_iota(jnp.int32, (tm, LANES), 0)
            for s in range(SLAB):
                rows = pl.ds(s, tm, stride=SLAB)
                y_ref[rows, :] = jnp.where(row >= lo, acc_ref[:, s * LANES:(s + 1) * LANES], y_ref[rows, :])


def _moe(xs, step_tile, step_expert, step_lo, step_hi, w1, w3, w2, tm):
    n_steps = step_tile.shape[0]
    w_spec = lambda shape: pl.BlockSpec((1,) + shape[1:], lambda s, tile, ex, lo, hi: (ex[s], 0, 0))
    row_spec = pl.BlockSpec((tm * SLAB, LANES), lambda s, tile, ex, lo, hi: (tile[s], 0))
    grid_spec = pltpu.PrefetchScalarGridSpec(
        num_scalar_prefetch=4,
        grid=(n_steps,),
        in_specs=[row_spec, w_spec(w1.shape), w_spec(w3.shape), w_spec(w2.shape)],
        out_specs=row_spec,
        scratch_shapes=[pltpu.VMEM((tm, D_MODEL), BF16), pltpu.VMEM((tm, D_MODEL), F32)],
    )
    return pl.pallas_call(
        _moe_kernel,
        grid_spec=grid_spec,
        out_shape=jax.ShapeDtypeStruct(xs.shape, F32),
        compiler_params=_params("arbitrary"),
        name="moe_grouped_swiglu",
    )(step_tile, step_expert, step_lo, step_hi, xs, w1, w3, w2)


def _combine_kernel(pos_ref, pos_next_ref, x_ref, gate_ref, g_ref, beta_ref, y_hbm, out_ref, ybuf, ffn_ref, sem):
    tm = x_ref.shape[0]
    i = pl.program_id(0)
    slot = i % 2

    def start_tile(p_ref, buf):
        def start(r):
            for k in range(2):
                src = pl.multiple_of(p_ref[0, 0, k * tm + r] * SLAB, SLAB)
                pltpu.make_async_copy(y_hbm.at[pl.ds(src, SLAB)],
                                      ybuf.at[buf, k, pl.ds(pl.multiple_of(r * SLAB, SLAB), SLAB)], sem.at[buf]).start()
        _for_rows(tm, start)

    @pl.when(i == 0)
    def _():
        start_tile(pos_ref, 0)

    @pl.when(i + 1 < pl.num_programs(0))
    def _():
        start_tile(pos_next_ref, 1 - slot)

    for k in range(2):
        pltpu.make_async_copy(y_hbm.at[pl.ds(0, tm * SLAB)], ybuf.at[slot, k], sem.at[slot]).wait()
    gate = gate_ref[...]
    for s in range(SLAB):
        rows = pl.ds(s, tm, stride=SLAB)
        ffn_ref[:, s * LANES:(s + 1) * LANES] = (gate[:, 0:1] * ybuf[slot, 0, rows, :]
                                                 + gate[:, 1:2] * ybuf[slot, 1, rows, :])
    out_ref[...] = _ln(ALPHA * x_ref[...] + ffn_ref[...], g_ref[...], beta_ref[...])


def _combine(x1, y, pos, gate, g, beta, tm):
    t = x1.shape[0]
    nt = t // tm
    pos_spec = lambda f: pl.BlockSpec((1, 1, 2 * tm), f, memory_space=pltpu.SMEM)
    return pl.pallas_call(
        _combine_kernel,
        grid=(nt,),
        in_specs=[pos_spec(lambda i: (i, 0, 0)), pos_spec(lambda i: (jnp.minimum(i + 1, nt - 1), 0, 0)),
                  _row_spec(tm, D_MODEL), _row_spec(tm, LANES), _full_spec(g.shape), _full_spec(beta.shape),
                  pl.BlockSpec(memory_space=pl.ANY)],
        out_specs=_row_spec(tm, D_MODEL),
        out_shape=jax.ShapeDtypeStruct((t, D_MODEL), F32),
        scratch_shapes=[pltpu.VMEM((2, 2, tm * SLAB, LANES), F32), pltpu.VMEM((tm, D_MODEL), F32),
                        pltpu.SemaphoreType.DMA((2,))],
        compiler_params=_params("arbitrary"),
        name="moe_combine_ln",
    )(pos, pos, x1, gate, g, beta, y)


def _routing_tables(route, cnt, t, tm):
    counts = cnt[0, :N_EXPERTS]
    ends = jnp.cumsum(counts)
    starts = ends - counts
    experts = jnp.arange(N_EXPERTS, dtype=jnp.int32)

    def sorted_row(e, rank):
        return jnp.sum(jnp.where(e[:, None] == experts[None, :], starts[None, :], 0), axis=1) + rank

    pos = jnp.stack([sorted_row(route[:, 0], route[:, 2]), sorted_row(route[:, 1], route[:, 3])])
    nt = t // tm
    pos = pos.reshape(2, nt, tm).transpose(1, 0, 2).reshape(nt, 1, 2 * tm)

    n_steps = (2 * t) // tm + N_EXPERTS - 1
    first_tile = starts // tm
    n_visits = jnp.where(counts > 0, (ends - 1) // tm - first_tile + 1, 0)
    step_ends = jnp.cumsum(n_visits)
    total = step_ends[-1]
    s = jnp.minimum(jnp.arange(n_steps, dtype=jnp.int32), total - 1)
    step_expert = jnp.sum((s[:, None] >= step_ends[None, :]).astype(jnp.int32), axis=1)
    step_tile = first_tile[step_expert] + s - (step_ends - n_visits)[step_expert]
    lo = jnp.clip(starts[step_expert] - step_tile * tm, 0, tm)
    hi = jnp.clip(ends[step_expert] - step_tile * tm, 0, tm)
    hi = jnp.where(jnp.arange(n_steps) < total, hi, lo)
    as_i32 = lambda a: a.astype(jnp.int32)
    return pos.astype(jnp.int32), as_i32(step_tile), as_i32(step_expert), as_i32(lo), as_i32(hi)


def _row(v):
    return v.reshape(1, -1).astype(F32)


def _even_layer(x, bsz, s, w_in, gate_bias, conv_w, conv_b, norm_g, norm_b, head_g, w_out,
                ln1_g, ln1_b, w1, w3, w2, ln2_g, ln2_b):
    main = 6 * HALF
    w_main = w_in[:, :main].astype(BF16)
    w_gate = jnp.pad(w_in[:, main:], ((0, 0), (0, LANES - 2 * MLSTM_HEADS))).astype(BF16)
    gb = jnp.pad(gate_bias, (0, LANES - 2 * MLSTM_HEADS)).reshape(1, LANES)
    glu, q, k, v, o, gates = _even_in(x, w_main, w_gate, gb, TOKEN_TILE)
    shp = lambda a: a.reshape(bsz, s, a.shape[-1])
    a = _conv_branch(shp(glu), conv_w, conv_b, _row(norm_g), _row(norm_b), min(s, 512))
    hb = _mlstm_branch(shp(q), shp(k), shp(v), shp(o), shp(gates), _row(head_g))
    x1 = _out_ln(a.reshape(-1, HALF), hb.reshape(-1, HALF), x, w_out.astype(BF16), _row(ln1_g), _row(ln1_b),
                 TOKEN_TILE)
    return _ffn(x1, w1.astype(BF16), w3.astype(BF16), w2.astype(BF16), _row(ln2_g), _row(ln2_b), TOKEN_TILE)


def _odd_layer(x, bsz, s, w_in, v_g, v_b, w_s, b_s, w_out, ln1_g, ln1_b, router_w, router_b,
               w1, w3, w2, ln2_g, ln2_b):
    t = x.shape[0]
    c, q, k, v = _odd_in(x, w_in.astype(BF16), _row(v_g), _row(v_b), w_s, b_s.T, TOKEN_TILE)
    shp = lambda a: a.reshape(bsz, s, HALF)
    d = _sb_attention(shp(q), shp(k), shp(v), min(s, SB_TILE)).reshape(t, HALF)
    rw = jnp.pad(router_w, ((0, 0), (0, LANES - N_EXPERTS)))
    rw_hi = rw.astype(BF16)
    rw_lo = (rw - rw_hi.astype(F32)).astype(BF16)
    rb = jnp.pad(router_b, (0, LANES - N_EXPERTS)).reshape(1, LANES)
    x1, gate, route, cnt = _odd_out(c, d, x, w_out.astype(BF16), _row(ln1_g), _row(ln1_b),
                                    jnp.stack([rw_hi, rw_lo]), rb, TOKEN_TILE)
    pos, step_tile, step_expert, step_lo, step_hi = _routing_tables(route, cnt, t, MOE_TILE)
    xs = _dispatch(x1, pos, MOE_TILE)
    y = _moe(xs, step_tile, step_expert, step_lo, step_hi,
             w1.astype(BF16), w3.astype(BF16), w2.astype(BF16), MOE_TILE)
    return _combine(x1, y, pos, gate, _row(ln2_g), _row(ln2_b), MOE_TILE)


def kernel(x, ab_w_in, ab_gate_bias, a_conv_w, a_conv_b, a_norm_g, a_norm_b, b_norm_g, ab_w_out, ab_ln1_g, ab_ln1_b, ffn_w1, ffn_w3, ffn_w2, ab_ln2_g, ab_ln2_b, cd_w_in, c_norm_g, c_norm_b, c_w_s, c_b_s, cd_w_out, cd_ln1_g, cd_ln1_b, router_w, router_b, moe_w1, moe_w3, moe_w2, cd_ln2_g, cd_ln2_b):
    bsz, s, _ = x.shape
    h = x.reshape(bsz * s, D_MODEL)
    for layer in range(DEPTH):
        j = layer // 2
        if layer % 2 == 0:
            h = _even_layer(h, bsz, s, ab_w_in[j], ab_gate_bias[j], a_conv_w[j], a_conv_b[j], a_norm_g[j],
                            a_norm_b[j], b_norm_g[j], ab_w_out[j], ab_ln1_g[j], ab_ln1_b[j],
                            ffn_w1[j], ffn_w3[j], ffn_w2[j], ab_ln2_g[j], ab_ln2_b[j])
        else:
            h = _odd_layer(h, bsz, s, cd_w_in[j], c_norm_g[j], c_norm_b[j], c_w_s[j], c_b_s[j], cd_w_out[j],
                           cd_ln1_g[j], cd_ln1_b[j], router_w[j], router_b[j],
                           moe_w1[j], moe_w3[j], moe_w2[j], cd_ln2_g[j], cd_ln2_b[j])
    return h.reshape(bsz, s, D_MODEL)
```

```python
import jax
import jax.numpy as jnp
from jax import lax
from jax.experimental import pallas as pl
from jax.experimental.pallas import tpu as pltpu

F32 = jnp.float32
BF16 = jnp.bfloat16

D_MODEL = 1024
DEPTH = 4
HALF = D_MODEL // 2
CONV_WIDTH = 31
CONV_ROWS = 16
CONV_HALO = 32
MLSTM_HEADS = 4
MLSTM_HEAD_DIM = HALF // MLSTM_HEADS
CHUNK = 128
GMLP_GROUPS = 4
SB_HEADS = 8
SB_HEAD_DIM = HALF // SB_HEADS
D_FF = 2816
FF_CHUNK = 512
N_EXPERTS = 8
ALPHA = (2 * DEPTH) ** 0.25
LN_EPS = 1e-5
LOG2E = 1.4426950408889634
MASKED_SCORE = -1e30
LANES = 128
SLAB = D_MODEL // LANES
V7X_VMEM_LIMIT = 56 * 1024 * 1024

TOKEN_TILE = 512
MOE_TILE = 512
SB_TILE = 512
SB_BATCH = 2
SB_KEY_TILE = 256


def _ln(x, g, b):
    mu = jnp.mean(x, axis=-1, keepdims=True)
    xc = x - mu
    var = jnp.mean(xc * xc, axis=-1, keepdims=True)
    return xc * lax.rsqrt(var + LN_EPS) * g + b


def _gelu(x):
    return 0.5 * x * (1.0 + lax.erf(x * (0.5 ** 0.5)))


def _dot(a, b):
    return jnp.dot(a, b, preferred_element_type=F32)


def _dot_nt(a, b):
    return lax.dot_general(a, b, (((1,), (1,)), ((), ())), preferred_element_type=F32)


def _split_bf16(x):
    hi = x.astype(BF16)
    lo = (x - hi.astype(F32)).astype(BF16)
    return hi, lo


def _params(*sem):
    return pltpu.CompilerParams(dimension_semantics=sem, vmem_limit_bytes=V7X_VMEM_LIMIT)


def _row_spec(tm, width):
    return pl.BlockSpec((tm, width), lambda i: (i, 0))


def _full_spec(shape):
    zeros = (0,) * len(shape)
    return pl.BlockSpec(shape, lambda *_: zeros)


def _even_in_kernel(x_ref, w_ref, wg_ref, gb_ref, glu_ref, q_ref, k_ref, v_ref, o_ref, g_ref, kt_ref, gt_ref):
    xb = x_ref[...].astype(BF16)

    def proj(c):
        return _dot(xb, w_ref[:, c * HALF:(c + 1) * HALF])

    glu_ref[...] = proj(0) * jax.nn.sigmoid(proj(1))
    q_ref[...] = proj(2).astype(BF16)
    k = proj(3) * MLSTM_HEAD_DIM ** -0.5
    k_ref[...] = k.astype(BF16)
    kt_ref[...] = k.T.astype(BF16)
    v_ref[...] = proj(4).astype(BF16)
    o_ref[...] = proj(5).astype(BF16)
    gates = _dot(xb, wg_ref[...]) + gb_ref[...]
    g_ref[...] = gates
    gt_ref[...] = gates.T[0:8, :]


def _even_in(x, w, wg, gb, tm):
    t = x.shape[0]
    half_bf = jax.ShapeDtypeStruct((t, HALF), BF16)
    return pl.pallas_call(
        _even_in_kernel,
        grid=(t // tm,),
        in_specs=[_row_spec(tm, D_MODEL), _full_spec(w.shape), _full_spec(wg.shape), _full_spec(gb.shape)],
        out_specs=[_row_spec(tm, HALF)] * 5 + [_row_spec(tm, LANES),
                   pl.BlockSpec((HALF, tm), lambda i: (0, i)), pl.BlockSpec((8, tm), lambda i: (0, i))],
        out_shape=[jax.ShapeDtypeStruct((t, HALF), F32), half_bf, half_bf, half_bf, half_bf,
                   jax.ShapeDtypeStruct((t, LANES), F32),
                   jax.ShapeDtypeStruct((HALF, t), BF16), jax.ShapeDtypeStruct((8, t), F32)],
        compiler_params=_params("parallel"),
        name="even_in_proj",
    )(x, w, wg, gb)


def _conv_kernel(prev_ref, cur_ref, w_ref, cb_ref, g_ref, b_ref, out_ref, win_ref, y_ref):
    nseq = cur_ref.shape[0]
    ts = cur_ref.shape[1]
    nblk = HALF // LANES
    first = pl.program_id(1) == 0
    for b in range(nseq):
        for c in range(nblk):
            lanes = slice(c * LANES, (c + 1) * LANES)
            sub = b * nblk + c
            win_ref[pl.ds(sub, CONV_HALO, stride=SLAB), :] = jnp.where(first, 0.0, prev_ref[b, :, lanes])
            win_ref[pl.ds(CONV_HALO * SLAB + sub, ts, stride=SLAB), :] = cur_ref[b, :, lanes]
    lead = CONV_HALO - (CONV_WIDTH - 1)
    rows = CONV_ROWS * SLAB

    def body(n, carry):
        base = pl.multiple_of(n * rows, rows)
        acc = jnp.tile(cb_ref[...], (CONV_ROWS, 1))
        for j in range(CONV_WIDTH):
            acc = acc + jnp.tile(w_ref[j], (CONV_ROWS, 1)) * win_ref[pl.ds(base + (lead + j) * SLAB, rows), :]
        y_ref[pl.ds(base, rows), :] = acc
        return carry

    lax.fori_loop(0, ts // CONV_ROWS, body, 0)
    for b in range(nseq):
        y = jnp.concatenate([y_ref[pl.ds(b * nblk + c, ts, stride=SLAB), :] for c in range(nblk)], axis=1)
        yn = _ln(y, g_ref[...], b_ref[...])
        out_ref[b] = (yn * jax.nn.sigmoid(yn)).astype(BF16)


def _conv_branch(glu, w, cb, g, b, ts):
    bsz, s, _ = glu.shape
    nseq = SLAB * LANES // HALF
    assert bsz % nseq == 0
    per = ts // CONV_HALO
    w_tiles = jnp.tile(jnp.pad(w, ((0, 32 - CONV_WIDTH), (0, 0))).reshape(32, HALF // LANES, LANES), (1, nseq, 1))
    cb_tile = jnp.tile(cb.reshape(HALF // LANES, LANES), (nseq, 1))
    return pl.pallas_call(
        _conv_kernel,
        grid=(bsz // nseq, s // ts),
        in_specs=[
            pl.BlockSpec((nseq, CONV_HALO, HALF), lambda bi, i: (bi, jnp.maximum(i * per - 1, 0), 0)),
            pl.BlockSpec((nseq, ts, HALF), lambda bi, i: (bi, i, 0)),
            _full_spec(w_tiles.shape), _full_spec(cb_tile.shape), _full_spec(g.shape), _full_spec(b.shape),
        ],
        out_specs=pl.BlockSpec((nseq, ts, HALF), lambda bi, i: (bi, i, 0)),
        out_shape=jax.ShapeDtypeStruct((bsz, s, HALF), BF16),
        scratch_shapes=[pltpu.VMEM(((ts + CONV_HALO) * SLAB, LANES), F32), pltpu.VMEM((ts * SLAB, LANES), F32)],
        compiler_params=_params("parallel", "parallel"),
        name="conv_branch",
    )(glu, glu, w_tiles, cb_tile, g, b)


def _mlstm_kernel(q_ref, k_ref, kt_ref, v_ref, o_ref, g_ref, gt_ref, hg_ref, out_ref, c_ref, n_ref, m_ref):
    L, H, d = CHUNK, MLSTM_HEADS, MLSTM_HEAD_DIM

    @pl.when(pl.program_id(1) == 0)
    def _():
        c_ref[...] = jnp.zeros_like(c_ref)
        n_ref[...] = jnp.zeros_like(n_ref)
        m_ref[...] = jnp.zeros_like(m_ref)

    row = lax.broadcasted_iota(jnp.int32, (L, L), 0)
    col = lax.broadcasted_iota(jnp.int32, (L, L), 1)
    causal = col <= row
    gates = g_ref[0]
    lf_hi, lf_lo = _split_bf16(jax.nn.log_sigmoid(gates))
    lower = jnp.where(causal, 1.0, 0.0).astype(BF16)
    bcum = _dot(lower, lf_hi) + _dot(lower, lf_lo)
    gates_r = gt_ref[...]
    lr_hi, lr_lo = _split_bf16(jax.nn.log_sigmoid(gates_r))
    upper = jnp.where(row <= col, 1.0, 0.0).astype(BF16)
    bcum_r = _dot(lr_hi, upper) + _dot(lr_lo, upper)

    for h in range(H):
        hs = slice(h * d, (h + 1) * d)
        qh, kh, vh = q_ref[0, :, hs], k_ref[0, :, hs], v_ref[0, :, hs]
        b_c = bcum[:, H + h:H + h + 1]
        i_r = gates_r[h:h + 1, :]
        b_r = bcum_r[H + h:H + h + 1, :]
        gtot = bcum[L - 1:L, H + h:H + h + 1]

        c_prev = c_ref[h]
        n_prev = n_ref[h, 0:1, :]
        m_prev = m_ref[h, 0:1, 0:1]

        log_w = gtot - b_r + i_r
        a = jnp.max(log_w, axis=1, keepdims=True)
        w = jnp.exp(log_w - a)
        kw_t = (kt_ref[hs, :].astype(F32) * w).astype(BF16)
        kv_loc = _dot(kw_t, vh)
        n_loc = _dot_nt(jnp.broadcast_to(w, (8, L)).astype(BF16), kt_ref[hs, :])[0:1, :]

        m_new = jnp.maximum(gtot + m_prev, a)
        s_old = jnp.exp(gtot + m_prev - m_new)
        s_new = jnp.exp(a - m_new)
        c_ref[h] = s_old * c_prev + s_new * kv_loc
        n_ref[h] = jnp.broadcast_to(s_old * n_prev + s_new * n_loc, (8, d))
        m_ref[h] = jnp.broadcast_to(m_new, (8, LANES))

        log_d = jnp.where(causal, b_c - b_r + i_r, -jnp.inf)
        log_inter = b_c + m_prev
        m_t = jnp.maximum(log_inter, jnp.max(log_d, axis=1, keepdims=True))
        d_intra = jnp.exp(log_d - m_t)
        s_inter = jnp.exp(log_inter - m_t)
        qk = _dot_nt(qh, kh) * d_intra
        num = _dot(qk.astype(BF16), vh) + s_inter * _dot(qh, c_prev.astype(BF16))
        den = jnp.sum(qk + s_inter * (qh.astype(F32) * n_prev), axis=1, keepdims=True)
        hh = num / jnp.maximum(jnp.abs(den), jnp.exp(-m_t))

        mu = jnp.mean(hh, axis=1, keepdims=True)
        hc = hh - mu
        var = jnp.mean(hc * hc, axis=1, keepdims=True)
        hn = hc * lax.rsqrt(var + LN_EPS) * hg_ref[:, hs]
        out_ref[0, :, hs] = (jax.nn.sigmoid(o_ref[0, :, hs].astype(F32)) * hn).astype(BF16)


def _mlstm_branch(q, k, kt, v, o, gates, gates_t, head_g):
    bsz, s, _ = q.shape
    nc = s // CHUNK
    blk = pl.BlockSpec((1, CHUNK, HALF), lambda bi, c: (bi, c, 0))
    return pl.pallas_call(
        _mlstm_kernel,
        grid=(bsz, nc),
        in_specs=[blk, blk, pl.BlockSpec((HALF, CHUNK), lambda bi, c: (0, bi * nc + c)), blk, blk,
                  pl.BlockSpec((1, CHUNK, LANES), lambda bi, c: (bi, c, 0)),
                  pl.BlockSpec((8, CHUNK), lambda bi, c: (0, bi * nc + c)),
                  _full_spec(head_g.shape)],
        out_specs=blk,
        out_shape=jax.ShapeDtypeStruct((bsz, s, HALF), BF16),
        scratch_shapes=[pltpu.VMEM((MLSTM_HEADS, MLSTM_HEAD_DIM, MLSTM_HEAD_DIM), F32),
                        pltpu.VMEM((MLSTM_HEADS, 8, MLSTM_HEAD_DIM), F32),
                        pltpu.VMEM((MLSTM_HEADS, 8, LANES), F32)],
        compiler_params=_params("parallel", "arbitrary"),
        name="mlstm_branch",
    )(q, k, kt, v, o, gates, gates_t, head_g)


def _out_ln_kernel(a_ref, b_ref, x_ref, w_ref, g_ref, beta_ref, out_ref):
    mix = _dot(a_ref[...], w_ref[0:HALF, :]) + _dot(b_ref[...], w_ref[HALF:, :])
    out_ref[...] = _ln(ALPHA * x_ref[...] + mix, g_ref[...], beta_ref[...])


def _out_ln(a, b, x, w, g, beta, tm):
    t = x.shape[0]
    return pl.pallas_call(
        _out_ln_kernel,
        grid=(t // tm,),
        in_specs=[_row_spec(tm, HALF), _row_spec(tm, HALF), _row_spec(tm, D_MODEL),
                  _full_spec(w.shape), _full_spec(g.shape), _full_spec(beta.shape)],
        out_specs=_row_spec(tm, D_MODEL),
        out_shape=jax.ShapeDtypeStruct((t, D_MODEL), F32),
        compiler_params=_params("parallel"),
        name="out_proj_ln",
    )(a, b, x, w, g, beta)


def _swiglu_into(xb_ref, w1_ref, w3_ref, w2_ref, acc_ref):
    for n, start in enumerate(range(0, D_FF, FF_CHUNK)):
        size = min(FF_CHUNK, D_FF - start)
        xb = xb_ref[...]
        h1 = _dot(xb, w1_ref[:, start:start + size])
        h3 = _dot(xb, w3_ref[:, start:start + size])
        hh = (h1 * jax.nn.sigmoid(h1) * h3).astype(BF16)
        out = _dot(hh, w2_ref[start:start + size, :])
        if n == 0:
            acc_ref[...] = out
        else:
            acc_ref[...] += out


def _ffn_kernel(x_ref, w1_ref, w3_ref, w2_ref, g_ref, beta_ref, out_ref, xb_ref, acc_ref):
    xb_ref[...] = x_ref[...].astype(BF16)
    _swiglu_into(xb_ref, w1_ref, w3_ref, w2_ref, acc_ref)
    out_ref[...] = _ln(ALPHA * x_ref[...] + acc_ref[...], g_ref[...], beta_ref[...])


def _ffn(x, w1, w3, w2, g, beta, tm):
    t = x.shape[0]
    return pl.pallas_call(
        _ffn_kernel,
        grid=(t // tm,),
        in_specs=[_row_spec(tm, D_MODEL), _full_spec(w1.shape), _full_spec(w3.shape), _full_spec(w2.shape),
                  _full_spec(g.shape), _full_spec(beta.shape)],
        out_specs=_row_spec(tm, D_MODEL),
        out_shape=jax.ShapeDtypeStruct((t, D_MODEL), F32),
        scratch_shapes=[pltpu.VMEM((tm, D_MODEL), BF16), pltpu.VMEM((tm, D_MODEL), F32)],
        compiler_params=_params("parallel"),
        name="dense_swiglu_ln",
    )(x, w1, w3, w2, g, beta)


def _odd_in_kernel(x_ref, w_ref, vg_ref, vb_ref, ws_ref, bs_ref, c_ref, q_ref, k_ref, v_ref):
    tm = x_ref.shape[0]
    xb = x_ref[...].astype(BF16)

    def proj(c):
        return _dot(xb, w_ref[:, c * HALF:(c + 1) * HALF])

    u = _gelu(proj(0))
    zn = _ln(_gelu(proj(1)), vg_ref[...], vb_ref[...]).astype(BF16)
    q_ref[...] = (proj(2) * SB_HEAD_DIM ** -0.5).astype(BF16)
    k_ref[...] = proj(3).astype(BF16)
    v_ref[...] = proj(4).astype(BF16)

    row = lax.broadcasted_iota(jnp.int32, (CHUNK, CHUNK), 0)
    col = lax.broadcasted_iota(jnp.int32, (CHUNK, CHUNK), 1)
    for g in range(GMLP_GROUPS):
        lanes = slice(g * LANES, (g + 1) * LANES)
        wc = jnp.where(col <= row, ws_ref[g], 0.0).astype(BF16)
        bias = bs_ref[:, g:g + 1]
        for r in range(tm // CHUNK):
            rows = slice(r * CHUNK, (r + 1) * CHUNK)
            sg = _dot(wc, zn[rows, lanes]) + bias
            c_ref[rows, lanes] = (u[rows, lanes] * sg).astype(BF16)


def _odd_in(x, w, vg, vb, ws, bs_t, tm):
    t = x.shape[0]
    half_bf = jax.ShapeDtypeStruct((t, HALF), BF16)
    return pl.pallas_call(
        _odd_in_kernel,
        grid=(t // tm,),
        in_specs=[_row_spec(tm, D_MODEL), _full_spec(w.shape), _full_spec(vg.shape), _full_spec(vb.shape),
                  _full_spec(ws.shape), _full_spec(bs_t.shape)],
        out_specs=[_row_spec(tm, HALF)] * 4,
        out_shape=[half_bf] * 4,
        compiler_params=_params("parallel"),
        name="odd_in_proj_gmlp",
    )(x, w, vg, vb, ws, bs_t)


def _sb_kernel(q_ref, k_ref, v_ref, out_ref, lsz_ref, sp_ref, run_ref, acc_ref):
    nb = q_ref.shape[0]
    tq = q_ref.shape[1]
    tk = SB_KEY_TILE
    ratio = tq // tk
    i = pl.program_id(2)
    lane = lax.broadcasted_iota(jnp.int32, (tq, LANES), 1)
    row = lax.broadcasted_iota(jnp.int32, (tk, tk), 0)
    col = lax.broadcasted_iota(jnp.int32, (tk, tk), 1)
    later = jnp.where(row > col, 1.0, 0.0).astype(BF16)
    qrow = lax.broadcasted_iota(jnp.int32, (tq, tk), 0)
    kcol = lax.broadcasted_iota(jnp.int32, (tq, tk), 1)
    last = ratio * (i + 1) - 1
    streams = []
    for b in range(nb):
        q = q_ref[b]
        zero = jnp.zeros_like(q)
        streams.append((b, 0, jnp.where(lane < SB_HEAD_DIM, q, zero)))
        streams.append((b, 1, jnp.where(lane >= SB_HEAD_DIM, q, zero)))

    def rows_of(k, diagonal):
        return slice((ratio - 1 - k) * tk, tq) if diagonal else slice(None)

    def scores(k, slot, diagonal):
        j = last - k
        rows = rows_of(k, diagonal)
        for b, h, qm in streams:
            kb = k_ref[b, pl.ds(pl.multiple_of(j * tk, tk), tk), :]
            z = _dot_nt(qm[rows], kb)
            sp = jnp.maximum(z, 0.0) + jnp.log(1.0 + jnp.exp2(jnp.abs(z) * -LOG2E))
            lsz = z - sp
            if diagonal:
                visible = (kcol + (ratio - 1 - k) * tk < qrow)[rows]
                sp = jnp.where(visible, sp, 0.0)
                lsz = jnp.where(visible, lsz, MASKED_SCORE)
            sp_ref[slot, b, h, rows] = sp.astype(BF16)
            run = run_ref[b, h, rows]
            for lt in range(tk // LANES):
                lanes = slice(lt * LANES, (lt + 1) * LANES)
                lsz_ref[slot, b, h, rows, lanes] = lsz[:, lanes] - run
            run_ref[b, h, rows] = run + jnp.sum(sp, axis=1, keepdims=True)

    def output(k, slot, diagonal=False):
        j = last - k
        rows = rows_of(k, diagonal)
        for b, h, _ in streams:
            vb = v_ref[b, pl.ds(pl.multiple_of(j * tk, tk), tk), :]
            att = jnp.exp2((lsz_ref[slot, b, h, rows] - _dot(sp_ref[slot, b, h, rows], later)) * LOG2E)
            acc_ref[b, h, rows] += _dot(att.astype(BF16), vb)

    run_ref[...] = jnp.zeros_like(run_ref)
    acc_ref[...] = jnp.zeros_like(acc_ref)
    scores(0, 0, True)
    for k in range(1, ratio):
        output(k - 1, (k - 1) % 2, True)
        scores(k, k % 2, True)

    def two_blocks(p, carry):
        output(2 * p - 1, 1)
        scores(2 * p, 0, False)
        output(2 * p, 0)
        scores(2 * p + 1, 1, False)
        return carry

    lax.fori_loop(ratio // 2, ratio * (i + 1) // 2, two_blocks, 0)
    output(last, 1)
    for b in range(nb):
        out_ref[b] = jnp.where(lane < SB_HEAD_DIM, acc_ref[b, 0], acc_ref[b, 1]).astype(BF16)


def _sb_attention(q, k, v, tq):
    bsz, s, _ = q.shape
    nb = SB_BATCH if bsz % SB_BATCH == 0 else 1
    kv_spec = pl.BlockSpec((nb, s, LANES), lambda bi, p, i: (bi, 0, p))
    q_spec = pl.BlockSpec((nb, tq, LANES), lambda bi, p, i: (bi, i, p))
    heads = LANES // SB_HEAD_DIM
    assert tq % (2 * SB_KEY_TILE) == 0, "the walk pairs key blocks, two scratch slots"
    return pl.pallas_call(
        _sb_kernel,
        grid=(bsz // nb, HALF // LANES, s // tq),
        in_specs=[q_spec, kv_spec, kv_spec],
        out_specs=q_spec,
        out_shape=jax.ShapeDtypeStruct((bsz, s, HALF), BF16),
        scratch_shapes=[pltpu.VMEM((2, nb, heads, tq, SB_KEY_TILE), F32),
                        pltpu.VMEM((2, nb, heads, tq, SB_KEY_TILE), BF16),
                        pltpu.VMEM((nb, heads, tq, LANES), F32), pltpu.VMEM((nb, heads, tq, LANES), F32)],
        compiler_params=_params("parallel", "parallel", "arbitrary"),
        name="stick_breaking_attention",
    )(q, k, v)


def _odd_out_kernel(c_ref, d_ref, x_ref, w_ref, g_ref, beta_ref, rw_ref, rb_ref,
                    x1_ref, gate_ref, route_ref, cnt_ref, base_ref):
    tm = x_ref.shape[0]

    @pl.when(pl.program_id(0) == 0)
    def _():
        base_ref[...] = jnp.zeros_like(base_ref)

    mix = _dot(c_ref[...], w_ref[0:HALF, :]) + _dot(d_ref[...], w_ref[HALF:, :])
    x1 = _ln(ALPHA * x_ref[...] + mix, g_ref[...], beta_ref[...])
    x1_ref[...] = x1

    xh, xl = _split_bf16(x1)
    logits = _dot(xh, rw_ref[0]) + _dot(xl, rw_ref[0]) + _dot(xh, rw_ref[1]) + rb_ref[...]
    lane = lax.broadcasted_iota(jnp.int32, (tm, LANES), 1)
    logits = jnp.where(lane < N_EXPERTS, logits, -jnp.inf)
    m1 = jnp.max(logits, axis=1, keepdims=True)
    i1 = jnp.min(jnp.where(logits == m1, lane, LANES), axis=1, keepdims=True)
    sel1 = lane == i1
    rest = jnp.where(sel1, -jnp.inf, logits)
    m2 = jnp.max(rest, axis=1, keepdims=True)
    i2 = jnp.min(jnp.where(rest == m2, lane, LANES), axis=1, keepdims=True)
    sel2 = lane == i2
    e = jnp.exp(m2 - m1)
    g1 = 1.0 / (1.0 + e)
    g2 = e / (1.0 + e)
    first_lower = i1 < i2
    gate_ref[...] = jnp.where(lane == 0, jnp.where(first_lower, g1, g2),
                              jnp.where(lane == 1, jnp.where(first_lower, g2, g1), 0.0))

    sel_f = jnp.where(sel1 | sel2, 1.0, 0.0)
    row = lax.broadcasted_iota(jnp.int32, (tm, tm), 0)
    col = lax.broadcasted_iota(jnp.int32, (tm, tm), 1)
    before = jnp.where(col < row, 1.0, 0.0).astype(BF16)
    rank = base_ref[0:1, :] + _dot(before, sel_f.astype(BF16))
    e_lo = jnp.minimum(i1, i2)
    e_hi = jnp.maximum(i1, i2)
    rank_lo = jnp.sum(jnp.where(lane == e_lo, rank, 0.0), axis=1, keepdims=True)
    rank_hi = jnp.sum(jnp.where(lane == e_hi, rank, 0.0), axis=1, keepdims=True)
    route_ref[...] = jnp.where(lane == 0, e_lo, jnp.where(lane == 1, e_hi, jnp.where(
        lane == 2, rank_lo.astype(jnp.int32), jnp.where(lane == 3, rank_hi.astype(jnp.int32), 0))))
    base_ref[...] = base_ref[...] + jnp.sum(sel_f, axis=0, keepdims=True)
    cnt_ref[...] = base_ref[...].astype(jnp.int32)


def _odd_out(c, d, x, w, g, beta, rw, rb, tm):
    t = x.shape[0]
    lane_spec = _row_spec(tm, LANES)
    return pl.pallas_call(
        _odd_out_kernel,
        grid=(t // tm,),
        in_specs=[_row_spec(tm, HALF), _row_spec(tm, HALF), _row_spec(tm, D_MODEL),
                  _full_spec(w.shape), _full_spec(g.shape), _full_spec(beta.shape),
                  _full_spec(rw.shape), _full_spec(rb.shape)],
        out_specs=[_row_spec(tm, D_MODEL), lane_spec, lane_spec, _full_spec((8, LANES))],
        out_shape=[jax.ShapeDtypeStruct((t, D_MODEL), F32), jax.ShapeDtypeStruct((t, LANES), F32),
                   jax.ShapeDtypeStruct((t, LANES), jnp.int32), jax.ShapeDtypeStruct((8, LANES), jnp.int32)],
        scratch_shapes=[pltpu.VMEM((8, LANES), F32)],
        compiler_params=_params("arbitrary"),
        name="odd_out_proj_ln_router",
    )(c, d, x, w, g, beta, rw, rb)


def _for_rows(n, fn):
    def body(r, carry):
        fn(r)
        return carry
    lax.fori_loop(0, n, body, 0, unroll=8)


def _dispatch_kernel(pos_ref, x_ref, xs_hbm, slab_ref, sem):
    tm = x_ref.shape[0]
    for s in range(SLAB):
        slab_ref[pl.ds(s, tm, stride=SLAB), :] = x_ref[:, s * LANES:(s + 1) * LANES]

    def start(r):
        src = slab_ref.at[pl.ds(pl.multiple_of(r * SLAB, SLAB), SLAB)]
        for k in range(2):
            dst = pl.multiple_of(pos_ref[0, 0, k * tm + r] * SLAB, SLAB)
            pltpu.make_async_copy(src, xs_hbm.at[pl.ds(dst, SLAB)], sem).start(priority=k)

    _for_rows(tm, start)
    for _ in range(2):
        pltpu.make_async_copy(slab_ref, xs_hbm.at[pl.ds(0, tm * SLAB)], sem).wait()


def _dispatch(x1, pos, tm):
    t = x1.shape[0]
    return pl.pallas_call(
        _dispatch_kernel,
        grid=(t // tm,),
        in_specs=[pl.BlockSpec((1, 1, 2 * tm), lambda i: (i, 0, 0), memory_space=pltpu.SMEM),
                  _row_spec(tm, D_MODEL)],
        out_specs=pl.BlockSpec(memory_space=pl.ANY),
        out_shape=jax.ShapeDtypeStruct((2 * t * SLAB, LANES), F32),
        scratch_shapes=[pltpu.VMEM((tm * SLAB, LANES), F32), pltpu.SemaphoreType.DMA(())],
        compiler_params=_params("arbitrary"),
        name="moe_dispatch",
    )(pos, x1)


def _moe_kernel(tile_ref, expert_ref, lo_ref, hi_ref, xs_ref, w1_ref, w3_ref, w2_ref, y_ref, xb_ref, acc_ref):
    tm = xb_ref.shape[0]
    step = pl.program_id(0)
    lo = lo_ref[step]
    hi = hi_ref[step]

    @pl.when(hi > lo)
    def _():
        for s in range(SLAB):
            xb_ref[:, s * LANES:(s + 1) * LANES] = xs_ref[pl.ds(s, tm, stride=SLAB), :].astype(BF16)
        _swiglu_into(xb_ref, w1_ref.at[0], w3_ref.at[0], w2_ref.at[0], acc_ref)

        @pl.when(lo == 0)
        def _():
            for s in range(SLAB):
                y_ref[pl.ds(s, tm, stride=SLAB), :] = acc_ref[:, s * LANES:(s + 1) * LANES]

        @pl.when(lo > 0)
        def _():
            row = lax.broadcasted_iota(jnp.int32, (tm, LANES), 0)
            for s in range(SLAB):
                rows = pl.ds(s, tm, stride=SLAB)
                y_ref[rows, :] = jnp.where(row >= lo, acc_ref[:, s * LANES:(s + 1) * LANES], y_ref[rows, :])


def _moe(xs, step_tile, step_expert, step_lo, step_hi, w1, w3, w2, tm):
    n_steps = step_tile.shape[0]
    w_spec = lambda shape: pl.BlockSpec((1,) + shape[1:], lambda s, tile, ex, lo, hi: (ex[s], 0, 0))
    row_spec = pl.BlockSpec((tm * SLAB, LANES), lambda s, tile, ex, lo, hi: (tile[s], 0))
    grid_spec = pltpu.PrefetchScalarGridSpec(
        num_scalar_prefetch=4,
        grid=(n_steps,),
        in_specs=[row_spec, w_spec(w1.shape), w_spec(w3.shape), w_spec(w2.shape)],
        out_specs=row_spec,
        scratch_shapes=[pltpu.VMEM((tm, D_MODEL), BF16), pltpu.VMEM((tm, D_MODEL), F32)],
    )
    return pl.pallas_call(
        _moe_kernel,
        grid_spec=grid_spec,
        out_shape=jax.ShapeDtypeStruct(xs.shape, F32),
        compiler_params=_params("arbitrary"),
        name="moe_grouped_swiglu",
    )(step_tile, step_expert, step_lo, step_hi, xs, w1, w3, w2)


def _combine_kernel(pos_ref, pos_next_ref, x_ref, gate_ref, g_ref, beta_ref, y_hbm, out_ref, ybuf, ffn_ref, sem):
    tm = x_ref.shape[0]
    i = pl.program_id(0)
    slot = i % 2

    def start_tile(p_ref, buf):
        def start(r):
            for k in range(2):
                src = pl.multiple_of(p_ref[0, 0, k * tm + r] * SLAB, SLAB)
                pltpu.make_async_copy(y_hbm.at[pl.ds(src, SLAB)],
                                      ybuf.at[buf, k, pl.ds(pl.multiple_of(r * SLAB, SLAB), SLAB)],
                                      sem.at[buf]).start(priority=k)
        _for_rows(tm, start)

    @pl.when(i == 0)
    def _():
        start_tile(pos_ref, 0)

    @pl.when(i + 1 < pl.num_programs(0))
    def _():
        start_tile(pos_next_ref, 1 - slot)

    for k in range(2):
        pltpu.make_async_copy(y_hbm.at[pl.ds(0, tm * SLAB)], ybuf.at[slot, k], sem.at[slot]).wait()
    gate = gate_ref[...]
    for s in range(SLAB):
        rows = pl.ds(s, tm, stride=SLAB)
        ffn_ref[:, s * LANES:(s + 1) * LANES] = (gate[:, 0:1] * ybuf[slot, 0, rows, :]
                                                 + gate[:, 1:2] * ybuf[slot, 1, rows, :])
    out_ref[...] = _ln(ALPHA * x_ref[...] + ffn_ref[...], g_ref[...], beta_ref[...])


def _combine(x1, y, pos, gate, g, beta, tm):
    t = x1.shape[0]
    nt = t // tm
    pos_spec = lambda f: pl.BlockSpec((1, 1, 2 * tm), f, memory_space=pltpu.SMEM)
    return pl.pallas_call(
        _combine_kernel,
        grid=(nt,),
        in_specs=[pos_spec(lambda i: (i, 0, 0)), pos_spec(lambda i: (jnp.minimum(i + 1, nt - 1), 0, 0)),
                  _row_spec(tm, D_MODEL), _row_spec(tm, LANES), _full_spec(g.shape), _full_spec(beta.shape),
                  pl.BlockSpec(memory_space=pl.ANY)],
        out_specs=_row_spec(tm, D_MODEL),
        out_shape=jax.ShapeDtypeStruct((t, D_MODEL), F32),
        scratch_shapes=[pltpu.VMEM((2, 2, tm * SLAB, LANES), F32), pltpu.VMEM((tm, D_MODEL), F32),
                        pltpu.SemaphoreType.DMA((2,))],
        compiler_params=_params("arbitrary"),
        name="moe_combine_ln",
    )(pos, pos, x1, gate, g, beta, y)


def _routing_tables(route, cnt, t, tm):
    counts = cnt[0, :N_EXPERTS]
    ends = jnp.cumsum(counts)
    starts = ends - counts
    experts = jnp.arange(N_EXPERTS, dtype=jnp.int32)

    def sorted_row(e, rank):
        return jnp.sum(jnp.where(e[:, None] == experts[None, :], starts[None, :], 0), axis=1) + rank

    pos = jnp.stack([sorted_row(route[:, 0], route[:, 2]), sorted_row(route[:, 1], route[:, 3])])
    nt = t // tm
    pos = pos.reshape(2, nt, tm).transpose(1, 0, 2).reshape(nt, 1, 2 * tm)

    n_steps = (2 * t) // tm + N_EXPERTS - 1
    first_tile = starts // tm
    n_visits = jnp.where(counts > 0, (ends - 1) // tm - first_tile + 1, 0)
    step_ends = jnp.cumsum(n_visits)
    total = step_ends[-1]
    s = jnp.minimum(jnp.arange(n_steps, dtype=jnp.int32), total - 1)
    step_expert = jnp.sum((s[:, None] >= step_ends[None, :]).astype(jnp.int32), axis=1)
    step_tile = first_tile[step_expert] + s - (step_ends - n_visits)[step_expert]
    lo = jnp.clip(starts[step_expert] - step_tile * tm, 0, tm)
    hi = jnp.clip(ends[step_expert] - step_tile * tm, 0, tm)
    hi = jnp.where(jnp.arange(n_steps) < total, hi, lo)
    as_i32 = lambda a: a.astype(jnp.int32)
    return pos.astype(jnp.int32), as_i32(step_tile), as_i32(step_expert), as_i32(lo), as_i32(hi)


def _row(v):
    return v.reshape(1, -1).astype(F32)


def _even_layer(x, bsz, s, w_in, gate_bias, conv_w, conv_b, norm_g, norm_b, head_g, w_out,
                ln1_g, ln1_b, w1, w3, w2, ln2_g, ln2_b):
    main = 6 * HALF
    w_main = w_in[:, :main].astype(BF16)
    w_gate = jnp.pad(w_in[:, main:], ((0, 0), (0, LANES - 2 * MLSTM_HEADS))).astype(BF16)
    gb = jnp.pad(gate_bias, (0, LANES - 2 * MLSTM_HEADS)).reshape(1, LANES)
    glu, q, k, v, o, gates, kt, gates_t = _even_in(x, w_main, w_gate, gb, TOKEN_TILE)
    shp = lambda a: a.reshape(bsz, s, a.shape[-1])
    a = _conv_branch(shp(glu), conv_w, conv_b, _row(norm_g), _row(norm_b), min(s, 512))
    hb = _mlstm_branch(shp(q), shp(k), kt, shp(v), shp(o), shp(gates), gates_t, _row(head_g))
    x1 = _out_ln(a.reshape(-1, HALF), hb.reshape(-1, HALF), x, w_out.astype(BF16), _row(ln1_g), _row(ln1_b),
                 TOKEN_TILE)
    return _ffn(x1, w1.astype(BF16), w3.astype(BF16), w2.astype(BF16), _row(ln2_g), _row(ln2_b), TOKEN_TILE)


def _odd_layer(x, bsz, s, w_in, v_g, v_b, w_s, b_s, w_out, ln1_g, ln1_b, router_w, router_b,
               w1, w3, w2, ln2_g, ln2_b):
    t = x.shape[0]
    c, q, k, v = _odd_in(x, w_in.astype(BF16), _row(v_g), _row(v_b), w_s, b_s.T, TOKEN_TILE)
    shp = lambda a: a.reshape(bsz, s, HALF)
    d = _sb_attention(shp(q), shp(k), shp(v), min(s, SB_TILE)).reshape(t, HALF)
    rw = jnp.pad(router_w, ((0, 0), (0, LANES - N_EXPERTS)))
    rw_hi = rw.astype(BF16)
    rw_lo = (rw - rw_hi.astype(F32)).astype(BF16)
    rb = jnp.pad(router_b, (0, LANES - N_EXPERTS)).reshape(1, LANES)
    x1, gate, route, cnt = _odd_out(c, d, x, w_out.astype(BF16), _row(ln1_g), _row(ln1_b),
                                    jnp.stack([rw_hi, rw_lo]), rb, TOKEN_TILE)
    pos, step_tile, step_expert, step_lo, step_hi = _routing_tables(route, cnt, t, MOE_TILE)
    xs = _dispatch(x1, pos, MOE_TILE)
    y = _moe(xs, step_tile, step_expert, step_lo, step_hi,
             w1.astype(BF16), w3.astype(BF16), w2.astype(BF16), MOE_TILE)
    return _combine(x1, y, pos, gate, _row(ln2_g), _row(ln2_b), MOE_TILE)


def kernel(x, ab_w_in, ab_gate_bias, a_conv_w, a_conv_b, a_norm_g, a_norm_b, b_norm_g, ab_w_out, ab_ln1_g, ab_ln1_b, ffn_w1, ffn_w3, ffn_w2, ab_ln2_g, ab_ln2_b, cd_w_in, c_norm_g, c_norm_b, c_w_s, c_b_s, cd_w_out, cd_ln1_g, cd_ln1_b, router_w, router_b, moe_w1, moe_w3, moe_w2, cd_ln2_g, cd_ln2_b):
    bsz, s, _ = x.shape
    h = x.reshape(bsz * s, D_MODEL)
    for layer in range(DEPTH):
        j = layer // 2
        if layer % 2 == 0:
            h = _even_layer(h, bsz, s, ab_w_in[j], ab_gate_bias[j], a_conv_w[j], a_conv_b[j], a_norm_g[j],
                            a_norm_b[j], b_norm_g[j], ab_w_out[j], ab_ln1_g[j], ab_ln1_b[j],
                            ffn_w1[j], ffn_w3[j], ffn_w2[j], ab_ln2_g[j], ab_ln2_b[j])
        else:
            h = _odd_layer(h, bsz, s, cd_w_in[j], c_norm_g[j], c_norm_b[j], c_w_s[j], c_b_s[j], cd_w_out[j],
                           cd_ln1_g[j], cd_ln1_b[j], router_w[j], router_b[j],
                           moe_w1[j], moe_w3[j], moe_w2[j], cd_ln2_g[j], cd_ln2_b[j])
    return h.reshape(bsz, s, D_MODEL)
```

```python
import jax
import jax.numpy as jnp
from jax import lax
from jax.experimental import pallas as pl
from jax.experimental.pallas import tpu as pltpu

F32 = jnp.float32
BF16 = jnp.bfloat16

D_MODEL = 1024
DEPTH = 4
HALF = D_MODEL // 2
CONV_WIDTH = 31
CONV_ROWS = 16
CONV_HALO = 32
MLSTM_HEADS = 4
MLSTM_HEAD_DIM = HALF // MLSTM_HEADS
CHUNK = 128
GMLP_GROUPS = 4
SB_HEADS = 8
SB_HEAD_DIM = HALF // SB_HEADS
D_FF = 2816
FF_CHUNK = 512
N_EXPERTS = 8
ALPHA = (2 * DEPTH) ** 0.25
LN_EPS = 1e-5
LOG2E = 1.4426950408889634
MASKED_SCORE = -1e30
LANES = 128
SLAB = D_MODEL // LANES
V7X_VMEM_LIMIT = 56 * 1024 * 1024

TOKEN_TILE = 512
MOE_TILE = 512
SB_TILE = 512
SB_BATCH = 2
SB_KEY_TILE = 256


def _ln(x, g, b):
    mu = jnp.mean(x, axis=-1, keepdims=True)
    xc = x - mu
    var = jnp.mean(xc * xc, axis=-1, keepdims=True)
    return xc * lax.rsqrt(var + LN_EPS) * g + b


def _gelu(x):
    return 0.5 * x * (1.0 + lax.erf(x * (0.5 ** 0.5)))


def _dot(a, b):
    return jnp.dot(a, b, preferred_element_type=F32)


def _dot_nt(a, b):
    return lax.dot_general(a, b, (((1,), (1,)), ((), ())), preferred_element_type=F32)


def _split_bf16(x):
    hi = x.astype(BF16)
    lo = (x - hi.astype(F32)).astype(BF16)
    return hi, lo


def _params(*sem):
    return pltpu.CompilerParams(dimension_semantics=sem, vmem_limit_bytes=V7X_VMEM_LIMIT)


def _row_spec(tm, width):
    return pl.BlockSpec((tm, width), lambda i: (i, 0))


def _full_spec(shape):
    zeros = (0,) * len(shape)
    return pl.BlockSpec(shape, lambda *_: zeros)


def _even_in_kernel(x_ref, w_ref, wg_ref, gb_ref, glu_ref, q_ref, k_ref, v_ref, o_ref, g_ref, kt_ref, gt_ref):
    xb = x_ref[...].astype(BF16)

    def proj(c):
        return _dot(xb, w_ref[:, c * HALF:(c + 1) * HALF])

    glu_ref[...] = proj(0) * jax.nn.sigmoid(proj(1))
    q_ref[...] = proj(2).astype(BF16)
    k = proj(3) * MLSTM_HEAD_DIM ** -0.5
    k_ref[...] = k.astype(BF16)
    kt_ref[...] = k.T.astype(BF16)
    v_ref[...] = proj(4).astype(BF16)
    o_ref[...] = proj(5).astype(BF16)
    gates = _dot(xb, wg_ref[...]) + gb_ref[...]
    g_ref[...] = gates
    gt_ref[...] = gates.T[0:8, :]


def _even_in(x, w, wg, gb, tm):
    t = x.shape[0]
    half_bf = jax.ShapeDtypeStruct((t, HALF), BF16)
    return pl.pallas_call(
        _even_in_kernel,
        grid=(t // tm,),
        in_specs=[_row_spec(tm, D_MODEL), _full_spec(w.shape), _full_spec(wg.shape), _full_spec(gb.shape)],
        out_specs=[_row_spec(tm, HALF)] * 5 + [_row_spec(tm, LANES),
                   pl.BlockSpec((HALF, tm), lambda i: (0, i)), pl.BlockSpec((8, tm), lambda i: (0, i))],
        out_shape=[jax.ShapeDtypeStruct((t, HALF), F32), half_bf, half_bf, half_bf, half_bf,
                   jax.ShapeDtypeStruct((t, LANES), F32),
                   jax.ShapeDtypeStruct((HALF, t), BF16), jax.ShapeDtypeStruct((8, t), F32)],
        compiler_params=_params("parallel"),
        name="even_in_proj",
    )(x, w, wg, gb)


def _conv_kernel(prev_ref, cur_ref, w_ref, cb_ref, g_ref, b_ref, out_ref, win_ref, y_ref):
    nseq = cur_ref.shape[0]
    ts = cur_ref.shape[1]
    nblk = HALF // LANES
    first = pl.program_id(1) == 0
    for b in range(nseq):
        for c in range(nblk):
            lanes = slice(c * LANES, (c + 1) * LANES)
            sub = b * nblk + c
            win_ref[pl.ds(sub, CONV_HALO, stride=SLAB), :] = jnp.where(first, 0.0, prev_ref[b, :, lanes])
            win_ref[pl.ds(CONV_HALO * SLAB + sub, ts, stride=SLAB), :] = cur_ref[b, :, lanes]
    lead = CONV_HALO - (CONV_WIDTH - 1)
    rows = CONV_ROWS * SLAB

    def body(n, carry):
        base = pl.multiple_of(n * rows, rows)
        acc = jnp.tile(cb_ref[...], (CONV_ROWS, 1))
        for j in range(CONV_WIDTH):
            acc = acc + jnp.tile(w_ref[j], (CONV_ROWS, 1)) * win_ref[pl.ds(base + (lead + j) * SLAB, rows), :]
        y_ref[pl.ds(base, rows), :] = acc
        return carry

    lax.fori_loop(0, ts // CONV_ROWS, body, 0)
    for b in range(nseq):
        y = jnp.concatenate([y_ref[pl.ds(b * nblk + c, ts, stride=SLAB), :] for c in range(nblk)], axis=1)
        yn = _ln(y, g_ref[...], b_ref[...])
        out_ref[b] = (yn * jax.nn.sigmoid(yn)).astype(BF16)


def _conv_branch(glu, w, cb, g, b, ts):
    bsz, s, _ = glu.shape
    nseq = SLAB * LANES // HALF
    assert bsz % nseq == 0
    per = ts // CONV_HALO
    w_tiles = jnp.tile(jnp.pad(w, ((0, 32 - CONV_WIDTH), (0, 0))).reshape(32, HALF // LANES, LANES), (1, nseq, 1))
    cb_tile = jnp.tile(cb.reshape(HALF // LANES, LANES), (nseq, 1))
    return pl.pallas_call(
        _conv_kernel,
        grid=(bsz // nseq, s // ts),
        in_specs=[
            pl.BlockSpec((nseq, CONV_HALO, HALF), lambda bi, i: (bi, jnp.maximum(i * per - 1, 0), 0)),
            pl.BlockSpec((nseq, ts, HALF), lambda bi, i: (bi, i, 0)),
            _full_spec(w_tiles.shape), _full_spec(cb_tile.shape), _full_spec(g.shape), _full_spec(b.shape),
        ],
        out_specs=pl.BlockSpec((nseq, ts, HALF), lambda bi, i: (bi, i, 0)),
        out_shape=jax.ShapeDtypeStruct((bsz, s, HALF), BF16),
        scratch_shapes=[pltpu.VMEM(((ts + CONV_HALO) * SLAB, LANES), F32), pltpu.VMEM((ts * SLAB, LANES), F32)],
        compiler_params=_params("parallel", "parallel"),
        name="conv_branch",
    )(glu, glu, w_tiles, cb_tile, g, b)


def _mlstm_kernel(q_ref, k_ref, kt_ref, v_ref, o_ref, g_ref, gt_ref, hg_ref, out_ref, c_ref, n_ref, m_ref):
    L, H, d = CHUNK, MLSTM_HEADS, MLSTM_HEAD_DIM

    @pl.when(pl.program_id(1) == 0)
    def _():
        c_ref[...] = jnp.zeros_like(c_ref)
        n_ref[...] = jnp.zeros_like(n_ref)
        m_ref[...] = jnp.zeros_like(m_ref)

    row = lax.broadcasted_iota(jnp.int32, (L, L), 0)
    col = lax.broadcasted_iota(jnp.int32, (L, L), 1)
    causal = col <= row
    gates = g_ref[0]
    lf_hi, lf_lo = _split_bf16(jax.nn.log_sigmoid(gates))
    lower = jnp.where(causal, 1.0, 0.0).astype(BF16)
    bcum = _dot(lower, lf_hi) + _dot(lower, lf_lo)
    gates_r = gt_ref[...]
    lr_hi, lr_lo = _split_bf16(jax.nn.log_sigmoid(gates_r))
    upper = jnp.where(row <= col, 1.0, 0.0).astype(BF16)
    bcum_r = _dot(lr_hi, upper) + _dot(lr_lo, upper)

    for h in range(H):
        hs = slice(h * d, (h + 1) * d)
        qh, kh, vh = q_ref[0, :, hs], k_ref[0, :, hs], v_ref[0, :, hs]
        b_c = bcum[:, H + h:H + h + 1]
        i_r = gates_r[h:h + 1, :]
        b_r = bcum_r[H + h:H + h + 1, :]
        gtot = bcum[L - 1:L, H + h:H + h + 1]

        c_prev = c_ref[h]
        n_prev = n_ref[h, 0:1, :]
        m_prev = m_ref[h, 0:1, 0:1]

        log_w = gtot - b_r + i_r
        a = jnp.max(log_w, axis=1, keepdims=True)
        w = jnp.exp(log_w - a)
        kw_t = (kt_ref[hs, :].astype(F32) * w).astype(BF16)
        kv_loc = _dot(kw_t, vh)
        n_loc = _dot_nt(jnp.broadcast_to(w, (8, L)).astype(BF16), kt_ref[hs, :])[0:1, :]

        m_new = jnp.maximum(gtot + m_prev, a)
        s_old = jnp.exp(gtot + m_prev - m_new)
        s_new = jnp.exp(a - m_new)
        c_ref[h] = s_old * c_prev + s_new * kv_loc
        n_ref[h] = jnp.broadcast_to(s_old * n_prev + s_new * n_loc, (8, d))
        m_ref[h] = jnp.broadcast_to(m_new, (8, LANES))

        log_d = jnp.where(causal, b_c - b_r + i_r, -jnp.inf)
        log_inter = b_c + m_prev
        m_t = jnp.maximum(log_inter, jnp.max(log_d, axis=1, keepdims=True))
        d_intra = jnp.exp(log_d - m_t)
        s_inter = jnp.exp(log_inter - m_t)
        qk = _dot_nt(qh, kh) * d_intra
        num = _dot(qk.astype(BF16), vh) + s_inter * _dot(qh, c_prev.astype(BF16))
        den = jnp.sum(qk + s_inter * (qh.astype(F32) * n_prev), axis=1, keepdims=True)
        hh = num / jnp.maximum(jnp.abs(den), jnp.exp(-m_t))

        mu = jnp.mean(hh, axis=1, keepdims=True)
        hc = hh - mu
        var = jnp.mean(hc * hc, axis=1, keepdims=True)
        hn = hc * lax.rsqrt(var + LN_EPS) * hg_ref[:, hs]
        out_ref[0, :, hs] = (jax.nn.sigmoid(o_ref[0, :, hs].astype(F32)) * hn).astype(BF16)


def _mlstm_branch(q, k, kt, v, o, gates, gates_t, head_g):
    bsz, s, _ = q.shape
    nc = s // CHUNK
    blk = pl.BlockSpec((1, CHUNK, HALF), lambda bi, c: (bi, c, 0))
    return pl.pallas_call(
        _mlstm_kernel,
        grid=(bsz, nc),
        in_specs=[blk, blk, pl.BlockSpec((HALF, CHUNK), lambda bi, c: (0, bi * nc + c)), blk, blk,
                  pl.BlockSpec((1, CHUNK, LANES), lambda bi, c: (bi, c, 0)),
                  pl.BlockSpec((8, CHUNK), lambda bi, c: (0, bi * nc + c)),
                  _full_spec(head_g.shape)],
        out_specs=blk,
        out_shape=jax.ShapeDtypeStruct((bsz, s, HALF), BF16),
        scratch_shapes=[pltpu.VMEM((MLSTM_HEADS, MLSTM_HEAD_DIM, MLSTM_HEAD_DIM), F32),
                        pltpu.VMEM((MLSTM_HEADS, 8, MLSTM_HEAD_DIM), F32),
                        pltpu.VMEM((MLSTM_HEADS, 8, LANES), F32)],
        compiler_params=_params("parallel", "arbitrary"),
        name="mlstm_branch",
    )(q, k, kt, v, o, gates, gates_t, head_g)


def _mix_ln(a_ref, b_ref, x_ref, w_ref, g_ref, beta_ref):
    mix = _dot(a_ref[...], w_ref[0:HALF, :]) + _dot(b_ref[...], w_ref[HALF:, :])
    return _ln(ALPHA * x_ref[...] + mix, g_ref[...], beta_ref[...])


def _layer_spec(w, layer):
    zeros = (0,) * (w.ndim - 1)
    return pl.BlockSpec((None,) + w.shape[1:], lambda *_: (layer,) + zeros)


def _swiglu_into(xb_ref, w1_ref, w3_ref, w2_ref, acc_ref):
    for n, start in enumerate(range(0, D_FF, FF_CHUNK)):
        size = min(FF_CHUNK, D_FF - start)
        xb = xb_ref[...]
        h1 = _dot(xb, w1_ref[:, start:start + size])
        h3 = _dot(xb, w3_ref[:, start:start + size])
        hh = (h1 * jax.nn.sigmoid(h1) * h3).astype(BF16)
        out = _dot(hh, w2_ref[start:start + size, :])
        if n == 0:
            acc_ref[...] = out
        else:
            acc_ref[...] += out


def _mix_ffn_kernel(a_ref, b_ref, x_ref, wo_ref, g1_ref, beta1_ref, w1_ref, w3_ref, w2_ref, g2_ref, beta2_ref,
                    out_ref, x1_ref, xb_ref, acc_ref):
    x1_ref[...] = _mix_ln(a_ref, b_ref, x_ref, wo_ref, g1_ref, beta1_ref)
    xb_ref[...] = x1_ref[...].astype(BF16)
    _swiglu_into(xb_ref, w1_ref, w3_ref, w2_ref, acc_ref)
    out_ref[...] = _ln(ALPHA * x1_ref[...] + acc_ref[...], g2_ref[...], beta2_ref[...])


def _mix_ffn(a, b, x, wo, g1, beta1, w1, w3, w2, layer, g2, beta2, tm):
    t = x.shape[0]
    return pl.pallas_call(
        _mix_ffn_kernel,
        grid=(t // tm,),
        in_specs=[_row_spec(tm, HALF), _row_spec(tm, HALF), _row_spec(tm, D_MODEL),
                  _full_spec(wo.shape), _full_spec(g1.shape), _full_spec(beta1.shape),
                  _layer_spec(w1, layer), _layer_spec(w3, layer), _layer_spec(w2, layer),
                  _full_spec(g2.shape), _full_spec(beta2.shape)],
        out_specs=_row_spec(tm, D_MODEL),
        out_shape=jax.ShapeDtypeStruct((t, D_MODEL), F32),
        scratch_shapes=[pltpu.VMEM((tm, D_MODEL), F32), pltpu.VMEM((tm, D_MODEL), BF16),
                        pltpu.VMEM((tm, D_MODEL), F32)],
        compiler_params=_params("parallel"),
        name="mix_ln_dense_swiglu_ln",
    )(a, b, x, wo, g1, beta1, w1, w3, w2, g2, beta2)


def _odd_in_kernel(x_ref, w_ref, vg_ref, vb_ref, ws_ref, bs_ref, c_ref, q_ref, k_ref, v_ref):
    tm = x_ref.shape[0]
    xb = x_ref[...].astype(BF16)

    def proj(c):
        return _dot(xb, w_ref[:, c * HALF:(c + 1) * HALF])

    u = _gelu(proj(0))
    zn = _ln(_gelu(proj(1)), vg_ref[...], vb_ref[...]).astype(BF16)
    q_ref[...] = (proj(2) * SB_HEAD_DIM ** -0.5).astype(BF16)
    k_ref[...] = proj(3).astype(BF16)
    v_ref[...] = proj(4).astype(BF16)

    row = lax.broadcasted_iota(jnp.int32, (CHUNK, CHUNK), 0)
    col = lax.broadcasted_iota(jnp.int32, (CHUNK, CHUNK), 1)
    for g in range(GMLP_GROUPS):
        lanes = slice(g * LANES, (g + 1) * LANES)
        wc = jnp.where(col <= row, ws_ref[g], 0.0).astype(BF16)
        bias = bs_ref[:, g:g + 1]
        for r in range(tm // CHUNK):
            rows = slice(r * CHUNK, (r + 1) * CHUNK)
            sg = _dot(wc, zn[rows, lanes]) + bias
            c_ref[rows, lanes] = (u[rows, lanes] * sg).astype(BF16)


def _odd_in(x, w, vg, vb, ws, bs_t, tm):
    t = x.shape[0]
    half_bf = jax.ShapeDtypeStruct((t, HALF), BF16)
    return pl.pallas_call(
        _odd_in_kernel,
        grid=(t // tm,),
        in_specs=[_row_spec(tm, D_MODEL), _full_spec(w.shape), _full_spec(vg.shape), _full_spec(vb.shape),
                  _full_spec(ws.shape), _full_spec(bs_t.shape)],
        out_specs=[_row_spec(tm, HALF)] * 4,
        out_shape=[half_bf] * 4,
        compiler_params=_params("parallel"),
        name="odd_in_proj_gmlp",
    )(x, w, vg, vb, ws, bs_t)


def _sb_kernel(q_ref, k_ref, v_ref, out_ref, lsz_ref, sp_ref, run_ref, acc_ref):
    nb = q_ref.shape[0]
    tq = q_ref.shape[1]
    tk = SB_KEY_TILE
    ratio = tq // tk
    i = pl.program_id(2)
    lane = lax.broadcasted_iota(jnp.int32, (tq, LANES), 1)
    row = lax.broadcasted_iota(jnp.int32, (tk, tk), 0)
    col = lax.broadcasted_iota(jnp.int32, (tk, tk), 1)
    later = jnp.where(row > col, 1.0, 0.0).astype(BF16)
    qrow = lax.broadcasted_iota(jnp.int32, (tq, tk), 0)
    kcol = lax.broadcasted_iota(jnp.int32, (tq, tk), 1)
    last = ratio * (i + 1) - 1
    streams = []
    for b in range(nb):
        q = q_ref[b]
        zero = jnp.zeros_like(q)
        streams.append((b, 0, jnp.where(lane < SB_HEAD_DIM, q, zero)))
        streams.append((b, 1, jnp.where(lane >= SB_HEAD_DIM, q, zero)))

    def rows_of(k, diagonal):
        return slice((ratio - 1 - k) * tk, tq) if diagonal else slice(None)

    def scores(k, slot, diagonal):
        j = last - k
        rows = rows_of(k, diagonal)
        for b, h, qm in streams:
            kb = k_ref[b, pl.ds(pl.multiple_of(j * tk, tk), tk), :]
            z = _dot_nt(qm[rows], kb)
            sp = jnp.maximum(z, 0.0) + jnp.log(1.0 + jnp.exp2(jnp.abs(z) * -LOG2E))
            lsz = z - sp
            if diagonal:
                visible = (kcol + (ratio - 1 - k) * tk < qrow)[rows]
                sp = jnp.where(visible, sp, 0.0)
                lsz = jnp.where(visible, lsz, MASKED_SCORE)
            sp_ref[slot, b, h, rows] = sp.astype(BF16)
            run = run_ref[b, h, rows]
            for lt in range(tk // LANES):
                lanes = slice(lt * LANES, (lt + 1) * LANES)
                lsz_ref[slot, b, h, rows, lanes] = lsz[:, lanes] - run
            run_ref[b, h, rows] = run + jnp.sum(sp, axis=1, keepdims=True)

    def output(k, slot, diagonal=False):
        j = last - k
        rows = rows_of(k, diagonal)
        for b, h, _ in streams:
            vb = v_ref[b, pl.ds(pl.multiple_of(j * tk, tk), tk), :]
            att = jnp.exp2((lsz_ref[slot, b, h, rows] - _dot(sp_ref[slot, b, h, rows], later)) * LOG2E)
            acc_ref[b, h, rows] += _dot(att.astype(BF16), vb)

    run_ref[...] = jnp.zeros_like(run_ref)
    acc_ref[...] = jnp.zeros_like(acc_ref)
    scores(0, 0, True)
    for k in range(1, ratio):
        output(k - 1, (k - 1) % 2, True)
        scores(k, k % 2, True)

    def two_blocks(p, carry):
        output(2 * p - 1, 1)
        scores(2 * p, 0, False)
        output(2 * p, 0)
        scores(2 * p + 1, 1, False)
        return carry

    lax.fori_loop(ratio // 2, ratio * (i + 1) // 2, two_blocks, 0)
    output(last, 1)
    for b in range(nb):
        out_ref[b] = jnp.where(lane < SB_HEAD_DIM, acc_ref[b, 0], acc_ref[b, 1]).astype(BF16)


def _sb_attention(q, k, v, tq):
    bsz, s, _ = q.shape
    nb = SB_BATCH if bsz % SB_BATCH == 0 else 1
    kv_spec = pl.BlockSpec((nb, s, LANES), lambda bi, p, i: (bi, 0, p))
    q_spec = pl.BlockSpec((nb, tq, LANES), lambda bi, p, i: (bi, i, p))
    heads = LANES // SB_HEAD_DIM
    assert tq % (2 * SB_KEY_TILE) == 0, "the walk pairs key blocks, two scratch slots"
    return pl.pallas_call(
        _sb_kernel,
        grid=(bsz // nb, HALF // LANES, s // tq),
        in_specs=[q_spec, kv_spec, kv_spec],
        out_specs=q_spec,
        out_shape=jax.ShapeDtypeStruct((bsz, s, HALF), BF16),
        scratch_shapes=[pltpu.VMEM((2, nb, heads, tq, SB_KEY_TILE), F32),
                        pltpu.VMEM((2, nb, heads, tq, SB_KEY_TILE), BF16),
                        pltpu.VMEM((nb, heads, tq, LANES), F32), pltpu.VMEM((nb, heads, tq, LANES), F32)],
        compiler_params=_params("parallel", "parallel", "arbitrary"),
        name="stick_breaking_attention",
    )(q, k, v)


def _odd_out_kernel(c_ref, d_ref, x_ref, w_ref, g_ref, beta_ref, rw_ref, rb_ref,
                    x1_ref, gate_ref, route_ref, cnt_ref, base_ref):
    tm = x_ref.shape[0]

    @pl.when(pl.program_id(0) == 0)
    def _():
        base_ref[...] = jnp.zeros_like(base_ref)

    x1 = _mix_ln(c_ref, d_ref, x_ref, w_ref, g_ref, beta_ref)
    x1_ref[...] = x1

    xh, xl = _split_bf16(x1)
    logits = _dot(xh, rw_ref[0]) + _dot(xl, rw_ref[0]) + _dot(xh, rw_ref[1]) + rb_ref[...]
    lane = lax.broadcasted_iota(jnp.int32, (tm, LANES), 1)
    logits = jnp.where(lane < N_EXPERTS, logits, -jnp.inf)
    m1 = jnp.max(logits, axis=1, keepdims=True)
    i1 = jnp.min(jnp.where(logits == m1, lane, LANES), axis=1, keepdims=True)
    sel1 = lane == i1
    rest = jnp.where(sel1, -jnp.inf, logits)
    m2 = jnp.max(rest, axis=1, keepdims=True)
    i2 = jnp.min(jnp.where(rest == m2, lane, LANES), axis=1, keepdims=True)
    sel2 = lane == i2
    e = jnp.exp(m2 - m1)
    g1 = 1.0 / (1.0 + e)
    g2 = e / (1.0 + e)
    first_lower = i1 < i2
    gate_ref[...] = jnp.where(lane == 0, jnp.where(first_lower, g1, g2),
                              jnp.where(lane == 1, jnp.where(first_lower, g2, g1), 0.0))

    sel_f = jnp.where(sel1 | sel2, 1.0, 0.0)
    row = lax.broadcasted_iota(jnp.int32, (tm, tm), 0)
    col = lax.broadcasted_iota(jnp.int32, (tm, tm), 1)
    before = jnp.where(col < row, 1.0, 0.0).astype(BF16)
    rank = base_ref[0:1, :] + _dot(before, sel_f.astype(BF16))
    e_lo = jnp.minimum(i1, i2)
    e_hi = jnp.maximum(i1, i2)
    rank_lo = jnp.sum(jnp.where(lane == e_lo, rank, 0.0), axis=1, keepdims=True)
    rank_hi = jnp.sum(jnp.where(lane == e_hi, rank, 0.0), axis=1, keepdims=True)
    route_ref[...] = jnp.where(lane == 0, e_lo, jnp.where(lane == 1, e_hi, jnp.where(
        lane == 2, rank_lo.astype(jnp.int32), jnp.where(lane == 3, rank_hi.astype(jnp.int32), 0))))
    base_ref[...] = base_ref[...] + jnp.sum(sel_f, axis=0, keepdims=True)
    cnt_ref[...] = base_ref[...].astype(jnp.int32)


def _odd_out(c, d, x, w, g, beta, rw, rb, tm):
    t = x.shape[0]
    lane_spec = _row_spec(tm, LANES)
    return pl.pallas_call(
        _odd_out_kernel,
        grid=(t // tm,),
        in_specs=[_row_spec(tm, HALF), _row_spec(tm, HALF), _row_spec(tm, D_MODEL),
                  _full_spec(w.shape), _full_spec(g.shape), _full_spec(beta.shape),
                  _full_spec(rw.shape), _full_spec(rb.shape)],
        out_specs=[_row_spec(tm, D_MODEL), lane_spec, lane_spec, _full_spec((8, LANES))],
        out_shape=[jax.ShapeDtypeStruct((t, D_MODEL), F32), jax.ShapeDtypeStruct((t, LANES), F32),
                   jax.ShapeDtypeStruct((t, LANES), jnp.int32), jax.ShapeDtypeStruct((8, LANES), jnp.int32)],
        scratch_shapes=[pltpu.VMEM((8, LANES), F32)],
        compiler_params=_params("arbitrary"),
        name="odd_out_proj_ln_router",
    )(c, d, x, w, g, beta, rw, rb)


def _for_rows(n, fn):
    def body(r, carry):
        fn(r)
        return carry
    lax.fori_loop(0, n, body, 0, unroll=8)


def _dispatch_kernel(pos_ref, x_ref, xs_hbm, slab_ref, sem):
    tm = x_ref.shape[0]
    for s in range(SLAB):
        slab_ref[pl.ds(s, tm, stride=SLAB), :] = x_ref[:, s * LANES:(s + 1) * LANES]

    def start(r):
        src = slab_ref.at[pl.ds(pl.multiple_of(r * SLAB, SLAB), SLAB)]
        for k in range(2):
            dst = pl.multiple_of(pos_ref[0, 0, k * tm + r] * SLAB, SLAB)
            pltpu.make_async_copy(src, xs_hbm.at[pl.ds(dst, SLAB)], sem).start(priority=k)

    _for_rows(tm, start)
    for _ in range(2):
        pltpu.make_async_copy(slab_ref, xs_hbm.at[pl.ds(0, tm * SLAB)], sem).wait()


def _dispatch(x1, pos, tm):
    t = x1.shape[0]
    return pl.pallas_call(
        _dispatch_kernel,
        grid=(t // tm,),
        in_specs=[pl.BlockSpec((1, 1, 2 * tm), lambda i: (i, 0, 0), memory_space=pltpu.SMEM),
                  _row_spec(tm, D_MODEL)],
        out_specs=pl.BlockSpec(memory_space=pl.ANY),
        out_shape=jax.ShapeDtypeStruct((2 * t * SLAB, LANES), F32),
        scratch_shapes=[pltpu.VMEM((tm * SLAB, LANES), F32), pltpu.SemaphoreType.DMA(())],
        compiler_params=_params("arbitrary"),
        name="moe_dispatch",
    )(pos, x1)


def _moe_kernel(tile_ref, expert_ref, lo_ref, hi_ref, xs_ref, w1_ref, w3_ref, w2_ref, y_ref, xb_ref, acc_ref):
    tm = xb_ref.shape[0]
    step = pl.program_id(0)
    lo = lo_ref[step]
    hi = hi_ref[step]

    @pl.when(hi > lo)
    def _():
        for s in range(SLAB):
            xb_ref[:, s * LANES:(s + 1) * LANES] = xs_ref[pl.ds(s, tm, stride=SLAB), :].astype(BF16)
        _swiglu_into(xb_ref, w1_ref, w3_ref, w2_ref, acc_ref)

        @pl.when(lo == 0)
        def _():
            for s in range(SLAB):
                y_ref[pl.ds(s, tm, stride=SLAB), :] = acc_ref[:, s * LANES:(s + 1) * LANES]

        @pl.when(lo > 0)
        def _():
            row = lax.broadcasted_iota(jnp.int32, (tm, LANES), 0)
            for s in range(SLAB):
                rows = pl.ds(s, tm, stride=SLAB)
                y_ref[rows, :] = jnp.where(row >= lo, acc_ref[:, s * LANES:(s + 1) * LANES], y_ref[rows, :])


def _moe(xs, step_tile, step_expert, step_lo, step_hi, w1, w3, w2, layer, tm):
    n_steps = step_tile.shape[0]
    w_spec = lambda shape: pl.BlockSpec((None, None) + shape[2:], lambda s, tile, ex, lo, hi: (layer, ex[s], 0, 0))
    row_spec = pl.BlockSpec((tm * SLAB, LANES), lambda s, tile, ex, lo, hi: (tile[s], 0))
    grid_spec = pltpu.PrefetchScalarGridSpec(
        num_scalar_prefetch=4,
        grid=(n_steps,),
        in_specs=[row_spec, w_spec(w1.shape), w_spec(w3.shape), w_spec(w2.shape)],
        out_specs=row_spec,
        scratch_shapes=[pltpu.VMEM((tm, D_MODEL), BF16), pltpu.VMEM((tm, D_MODEL), F32)],
    )
    return pl.pallas_call(
        _moe_kernel,
        grid_spec=grid_spec,
        out_shape=jax.ShapeDtypeStruct(xs.shape, F32),
        compiler_params=_params("arbitrary"),
        name="moe_grouped_swiglu",
    )(step_tile, step_expert, step_lo, step_hi, xs, w1, w3, w2)


def _combine_kernel(pos_ref, pos_next_ref, x_ref, gate_ref, g_ref, beta_ref, y_hbm, out_ref, ybuf, ffn_ref, sem):
    tm = x_ref.shape[0]
    i = pl.program_id(0)
    slot = i % 2

    def start_tile(p_ref, buf):
        def start(r):
            for k in range(2):
                src = pl.multiple_of(p_ref[0, 0, k * tm + r] * SLAB, SLAB)
                pltpu.make_async_copy(y_hbm.at[pl.ds(src, SLAB)],
                                      ybuf.at[buf, k, pl.ds(pl.multiple_of(r * SLAB, SLAB), SLAB)],
                                      sem.at[buf]).start(priority=k)
        _for_rows(tm, start)

    @pl.when(i == 0)
    def _():
        start_tile(pos_ref, 0)

    @pl.when(i + 1 < pl.num_programs(0))
    def _():
        start_tile(pos_next_ref, 1 - slot)

    for k in range(2):
        pltpu.make_async_copy(y_hbm.at[pl.ds(0, tm * SLAB)], ybuf.at[slot, k], sem.at[slot]).wait()
    gate = gate_ref[...]
    for s in range(SLAB):
        rows = pl.ds(s, tm, stride=SLAB)
        ffn_ref[:, s * LANES:(s + 1) * LANES] = (gate[:, 0:1] * ybuf[slot, 0, rows, :]
                                                 + gate[:, 1:2] * ybuf[slot, 1, rows, :])
    out_ref[...] = _ln(ALPHA * x_ref[...] + ffn_ref[...], g_ref[...], beta_ref[...])


def _combine(x1, y, pos, gate, g, beta, tm):
    t = x1.shape[0]
    nt = t // tm
    pos_spec = lambda f: pl.BlockSpec((1, 1, 2 * tm), f, memory_space=pltpu.SMEM)
    return pl.pallas_call(
        _combine_kernel,
        grid=(nt,),
        in_specs=[pos_spec(lambda i: (i, 0, 0)), pos_spec(lambda i: (jnp.minimum(i + 1, nt - 1), 0, 0)),
                  _row_spec(tm, D_MODEL), _row_spec(tm, LANES), _full_spec(g.shape), _full_spec(beta.shape),
                  pl.BlockSpec(memory_space=pl.ANY)],
        out_specs=_row_spec(tm, D_MODEL),
        out_shape=jax.ShapeDtypeStruct((t, D_MODEL), F32),
        scratch_shapes=[pltpu.VMEM((2, 2, tm * SLAB, LANES), F32), pltpu.VMEM((tm, D_MODEL), F32),
                        pltpu.SemaphoreType.DMA((2,))],
        compiler_params=_params("arbitrary"),
        name="moe_combine_ln",
    )(pos, pos, x1, gate, g, beta, y)


def _routing_tables(route, cnt, t, tm):
    counts = cnt[0, :N_EXPERTS]
    ends = jnp.cumsum(counts)
    starts = ends - counts
    experts = jnp.arange(N_EXPERTS, dtype=jnp.int32)

    def sorted_row(e, rank):
        return jnp.sum(jnp.where(e[:, None] == experts[None, :], starts[None, :], 0), axis=1) + rank

    pos = jnp.stack([sorted_row(route[:, 0], route[:, 2]), sorted_row(route[:, 1], route[:, 3])])
    nt = t // tm
    pos = pos.reshape(2, nt, tm).transpose(1, 0, 2).reshape(nt, 1, 2 * tm)

    n_steps = (2 * t) // tm + N_EXPERTS - 1
    first_tile = starts // tm
    n_visits = jnp.where(counts > 0, (ends - 1) // tm - first_tile + 1, 0)
    step_ends = jnp.cumsum(n_visits)
    total = step_ends[-1]
    s = jnp.minimum(jnp.arange(n_steps, dtype=jnp.int32), total - 1)
    step_expert = jnp.sum((s[:, None] >= step_ends[None, :]).astype(jnp.int32), axis=1)
    step_tile = first_tile[step_expert] + s - (step_ends - n_visits)[step_expert]
    lo = jnp.clip(starts[step_expert] - step_tile * tm, 0, tm)
    hi = jnp.clip(ends[step_expert] - step_tile * tm, 0, tm)
    hi = jnp.where(jnp.arange(n_steps) < total, hi, lo)
    as_i32 = lambda a: a.astype(jnp.int32)
    return pos.astype(jnp.int32), as_i32(step_tile), as_i32(step_expert), as_i32(lo), as_i32(hi)


def _row(v):
    return v.reshape(1, -1).astype(F32)


def _even_layer(x, bsz, s, w_in, gate_bias, conv_w, conv_b, norm_g, norm_b, head_g, w_out,
                ln1_g, ln1_b, w1, w3, w2, layer, ln2_g, ln2_b):
    main = 6 * HALF
    w_main = w_in[:, :main].astype(BF16)
    w_gate = jnp.pad(w_in[:, main:], ((0, 0), (0, LANES - 2 * MLSTM_HEADS))).astype(BF16)
    gb = jnp.pad(gate_bias, (0, LANES - 2 * MLSTM_HEADS)).reshape(1, LANES)
    glu, q, k, v, o, gates, kt, gates_t = _even_in(x, w_main, w_gate, gb, TOKEN_TILE)
    shp = lambda a: a.reshape(bsz, s, a.shape[-1])
    a = _conv_branch(shp(glu), conv_w, conv_b, _row(norm_g), _row(norm_b), min(s, 512))
    hb = _mlstm_branch(shp(q), shp(k), kt, shp(v), shp(o), shp(gates), gates_t, _row(head_g))
    return _mix_ffn(a.reshape(-1, HALF), hb.reshape(-1, HALF), x, w_out.astype(BF16), _row(ln1_g), _row(ln1_b),
                    w1, w3, w2, layer, _row(ln2_g), _row(ln2_b), TOKEN_TILE)


def _odd_layer(x, bsz, s, w_in, v_g, v_b, w_s, b_s, w_out, ln1_g, ln1_b, router_w, router_b,
               w1, w3, w2, layer, ln2_g, ln2_b):
    t = x.shape[0]
    c, q, k, v = _odd_in(x, w_in.astype(BF16), _row(v_g), _row(v_b), w_s, b_s.T, TOKEN_TILE)
    shp = lambda a: a.reshape(bsz, s, HALF)
    d = _sb_attention(shp(q), shp(k), shp(v), min(s, SB_TILE)).reshape(t, HALF)
    rw = jnp.pad(router_w, ((0, 0), (0, LANES - N_EXPERTS)))
    rw_hi = rw.astype(BF16)
    rw_lo = (rw - rw_hi.astype(F32)).astype(BF16)
    rb = jnp.pad(router_b, (0, LANES - N_EXPERTS)).reshape(1, LANES)
    x1, gate, route, cnt = _odd_out(c, d, x, w_out.astype(BF16), _row(ln1_g), _row(ln1_b),
                                    jnp.stack([rw_hi, rw_lo]), rb, TOKEN_TILE)
    pos, step_tile, step_expert, step_lo, step_hi = _routing_tables(route, cnt, t, MOE_TILE)
    xs = _dispatch(x1, pos, MOE_TILE)
    y = _moe(xs, step_tile, step_expert, step_lo, step_hi, w1, w3, w2, layer, MOE_TILE)
    return _combine(x1, y, pos, gate, _row(ln2_g), _row(ln2_b), MOE_TILE)


def kernel(x, ab_w_in, ab_gate_bias, a_conv_w, a_conv_b, a_norm_g, a_norm_b, b_norm_g, ab_w_out, ab_ln1_g, ab_ln1_b, ffn_w1, ffn_w3, ffn_w2, ab_ln2_g, ab_ln2_b, cd_w_in, c_norm_g, c_norm_b, c_w_s, c_b_s, cd_w_out, cd_ln1_g, cd_ln1_b, router_w, router_b, moe_w1, moe_w3, moe_w2, cd_ln2_g, cd_ln2_b):
    bsz, s, _ = x.shape
    h = x.reshape(bsz * s, D_MODEL)
    ffn_w = [w.astype(BF16) for w in (ffn_w1, ffn_w3, ffn_w2)]
    moe_w = [w.astype(BF16) for w in (moe_w1, moe_w3, moe_w2)]
    for layer in range(DEPTH):
        j = layer // 2
        if layer % 2 == 0:
            h = _even_layer(h, bsz, s, ab_w_in[j], ab_gate_bias[j], a_conv_w[j], a_conv_b[j], a_norm_g[j],
                            a_norm_b[j], b_norm_g[j], ab_w_out[j], ab_ln1_g[j], ab_ln1_b[j],
                            *ffn_w, j, ab_ln2_g[j], ab_ln2_b[j])
        else:
            h = _odd_layer(h, bsz, s, cd_w_in[j], c_norm_g[j], c_norm_b[j], c_w_s[j], c_b_s[j], cd_w_out[j],
                           cd_ln1_g[j], cd_ln1_b[j], router_w[j], router_b[j],
                           *moe_w, j, cd_ln2_g[j], cd_ln2_b[j])
    return h.reshape(bsz, s, D_MODEL)
```

```python
import jax
import jax.numpy as jnp
from jax import lax
from jax.experimental import pallas as pl
from jax.experimental.pallas import tpu as pltpu

F32 = jnp.float32
BF16 = jnp.bfloat16

D_MODEL = 1024
DEPTH = 4
HALF = D_MODEL // 2
CONV_WIDTH = 31
CONV_ROWS = 16
CONV_HALO = 32
MLSTM_HEADS = 4
MLSTM_HEAD_DIM = HALF // MLSTM_HEADS
CHUNK = 128
GMLP_GROUPS = 4
SB_HEADS = 8
SB_HEAD_DIM = HALF // SB_HEADS
D_FF = 2816
FF_CHUNK = 512
N_EXPERTS = 8
ALPHA = (2 * DEPTH) ** 0.25
LN_EPS = 1e-5
LOG2E = 1.4426950408889634
MASKED_SCORE = -1e30
LANES = 128
SLAB = D_MODEL // LANES
V7X_VMEM_LIMIT = 56 * 1024 * 1024

TOKEN_TILE = 512
MOE_TILE = 512
SB_TILE = 1024
SB_BATCH = 2
SB_KEY_TILE = 256


def _ln(x, g, b):
    mu = jnp.mean(x, axis=-1, keepdims=True)
    xc = x - mu
    var = jnp.mean(xc * xc, axis=-1, keepdims=True)
    return xc * lax.rsqrt(var + LN_EPS) * g + b


def _gelu(x):
    return 0.5 * x * (1.0 + lax.erf(x * (0.5 ** 0.5)))


def _dot(a, b):
    return jnp.dot(a, b, preferred_element_type=F32)


def _dot_nt(a, b):
    return lax.dot_general(a, b, (((1,), (1,)), ((), ())), preferred_element_type=F32)


def _split_bf16(x):
    hi = x.astype(BF16)
    lo = (x - hi.astype(F32)).astype(BF16)
    return hi, lo


def _params(*sem):
    return pltpu.CompilerParams(dimension_semantics=sem, vmem_limit_bytes=V7X_VMEM_LIMIT)


def _row_spec(tm, width):
    return pl.BlockSpec((tm, width), lambda i: (i, 0))


def _full_spec(shape):
    zeros = (0,) * len(shape)
    return pl.BlockSpec(shape, lambda *_: zeros)


def _even_in_kernel(x_ref, w_ref, wg_ref, gb_ref, glu_ref, q_ref, k_ref, v_ref, o_ref, g_ref, kt_ref, gt_ref):
    xb = x_ref[...].astype(BF16)

    def proj(c):
        return _dot(xb, w_ref[:, c * HALF:(c + 1) * HALF])

    glu_ref[...] = proj(0) * jax.nn.sigmoid(proj(1))
    q_ref[...] = proj(2).astype(BF16)
    k = proj(3) * MLSTM_HEAD_DIM ** -0.5
    k_ref[...] = k.astype(BF16)
    kt_ref[...] = k.T.astype(BF16)
    v_ref[...] = proj(4).astype(BF16)
    o_ref[...] = proj(5).astype(BF16)
    gates = _dot(xb, wg_ref[...]) + gb_ref[...]
    g_ref[...] = gates
    gt_ref[...] = gates.T[0:8, :]


def _even_in(x, w, wg, gb, tm):
    t = x.shape[0]
    half_bf = jax.ShapeDtypeStruct((t, HALF), BF16)
    return pl.pallas_call(
        _even_in_kernel,
        grid=(t // tm,),
        in_specs=[_row_spec(tm, D_MODEL), _full_spec(w.shape), _full_spec(wg.shape), _full_spec(gb.shape)],
        out_specs=[_row_spec(tm, HALF)] * 5 + [_row_spec(tm, LANES),
                   pl.BlockSpec((HALF, tm), lambda i: (0, i)), pl.BlockSpec((8, tm), lambda i: (0, i))],
        out_shape=[jax.ShapeDtypeStruct((t, HALF), F32), half_bf, half_bf, half_bf, half_bf,
                   jax.ShapeDtypeStruct((t, LANES), F32),
                   jax.ShapeDtypeStruct((HALF, t), BF16), jax.ShapeDtypeStruct((8, t), F32)],
        compiler_params=_params("parallel"),
        name="even_in_proj",
    )(x, w, wg, gb)


def _conv_kernel(prev_ref, cur_ref, w_ref, cb_ref, g_ref, b_ref, out_ref, win_ref, y_ref):
    nseq = cur_ref.shape[0]
    ts = cur_ref.shape[1]
    nblk = HALF // LANES
    first = pl.program_id(1) == 0
    for b in range(nseq):
        for c in range(nblk):
            lanes = slice(c * LANES, (c + 1) * LANES)
            sub = b * nblk + c
            win_ref[pl.ds(sub, CONV_HALO, stride=SLAB), :] = jnp.where(first, 0.0, prev_ref[b, :, lanes])
            win_ref[pl.ds(CONV_HALO * SLAB + sub, ts, stride=SLAB), :] = cur_ref[b, :, lanes]
    lead = CONV_HALO - (CONV_WIDTH - 1)
    rows = CONV_ROWS * SLAB

    def body(n, carry):
        base = pl.multiple_of(n * rows, rows)
        acc = jnp.tile(cb_ref[...], (CONV_ROWS, 1))
        for j in range(CONV_WIDTH):
            acc = acc + jnp.tile(w_ref[j], (CONV_ROWS, 1)) * win_ref[pl.ds(base + (lead + j) * SLAB, rows), :]
        y_ref[pl.ds(base, rows), :] = acc
        return carry

    lax.fori_loop(0, ts // CONV_ROWS, body, 0)
    for b in range(nseq):
        y = jnp.concatenate([y_ref[pl.ds(b * nblk + c, ts, stride=SLAB), :] for c in range(nblk)], axis=1)
        yn = _ln(y, g_ref[...], b_ref[...])
        out_ref[b] = (yn * jax.nn.sigmoid(yn)).astype(BF16)


def _conv_branch(glu, w, cb, g, b, ts):
    bsz, s, _ = glu.shape
    nseq = SLAB * LANES // HALF
    assert bsz % nseq == 0
    per = ts // CONV_HALO
    w_tiles = jnp.tile(jnp.pad(w, ((0, 32 - CONV_WIDTH), (0, 0))).reshape(32, HALF // LANES, LANES), (1, nseq, 1))
    cb_tile = jnp.tile(cb.reshape(HALF // LANES, LANES), (nseq, 1))
    return pl.pallas_call(
        _conv_kernel,
        grid=(bsz // nseq, s // ts),
        in_specs=[
            pl.BlockSpec((nseq, CONV_HALO, HALF), lambda bi, i: (bi, jnp.maximum(i * per - 1, 0), 0)),
            pl.BlockSpec((nseq, ts, HALF), lambda bi, i: (bi, i, 0)),
            _full_spec(w_tiles.shape), _full_spec(cb_tile.shape), _full_spec(g.shape), _full_spec(b.shape),
        ],
        out_specs=pl.BlockSpec((nseq, ts, HALF), lambda bi, i: (bi, i, 0)),
        out_shape=jax.ShapeDtypeStruct((bsz, s, HALF), BF16),
        scratch_shapes=[pltpu.VMEM(((ts + CONV_HALO) * SLAB, LANES), F32), pltpu.VMEM((ts * SLAB, LANES), F32)],
        compiler_params=_params("parallel", "parallel"),
        name="conv_branch",
    )(glu, glu, w_tiles, cb_tile, g, b)


def _mlstm_kernel(q_ref, k_ref, kt_ref, v_ref, o_ref, g_ref, gt_ref, hg_ref, out_ref, c_ref, n_ref, m_ref):
    L, H, d = CHUNK, MLSTM_HEADS, MLSTM_HEAD_DIM

    @pl.when(pl.program_id(1) == 0)
    def _():
        c_ref[...] = jnp.zeros_like(c_ref)
        n_ref[...] = jnp.zeros_like(n_ref)
        m_ref[...] = jnp.zeros_like(m_ref)

    row = lax.broadcasted_iota(jnp.int32, (L, L), 0)
    col = lax.broadcasted_iota(jnp.int32, (L, L), 1)
    causal = col <= row
    gates = g_ref[0]
    lf_hi, lf_lo = _split_bf16(jax.nn.log_sigmoid(gates))
    lower = jnp.where(causal, 1.0, 0.0).astype(BF16)
    bcum = _dot(lower, lf_hi) + _dot(lower, lf_lo)
    gates_r = gt_ref[...]
    lr_hi, lr_lo = _split_bf16(jax.nn.log_sigmoid(gates_r))
    upper = jnp.where(row <= col, 1.0, 0.0).astype(BF16)
    bcum_r = _dot(lr_hi, upper) + _dot(lr_lo, upper)

    for h in range(H):
        hs = slice(h * d, (h + 1) * d)
        qh, kh, vh = q_ref[0, :, hs], k_ref[0, :, hs], v_ref[0, :, hs]
        b_c = bcum[:, H + h:H + h + 1]
        i_r = gates_r[h:h + 1, :]
        b_r = bcum_r[H + h:H + h + 1, :]
        gtot = bcum[L - 1:L, H + h:H + h + 1]

        c_prev = c_ref[h]
        n_prev = n_ref[h, 0:1, :]
        m_prev = m_ref[h, 0:1, 0:1]

        log_w = gtot - b_r + i_r
        a = jnp.max(log_w, axis=1, keepdims=True)
        w = jnp.exp(log_w - a)
        kw_t = (kt_ref[hs, :].astype(F32) * w).astype(BF16)
        kv_loc = _dot(kw_t, vh)
        n_loc = _dot_nt(jnp.broadcast_to(w, (8, L)).astype(BF16), kt_ref[hs, :])[0:1, :]

        m_new = jnp.maximum(gtot + m_prev, a)
        s_old = jnp.exp(gtot + m_prev - m_new)
        s_new = jnp.exp(a - m_new)
        c_ref[h] = s_old * c_prev + s_new * kv_loc
        n_ref[h] = jnp.broadcast_to(s_old * n_prev + s_new * n_loc, (8, d))
        m_ref[h] = jnp.broadcast_to(m_new, (8, LANES))

        log_d = jnp.where(causal, b_c - b_r + i_r, -jnp.inf)
        log_inter = b_c + m_prev
        m_t = jnp.maximum(log_inter, jnp.max(log_d, axis=1, keepdims=True))
        d_intra = jnp.exp(log_d - m_t)
        s_inter = jnp.exp(log_inter - m_t)
        qk = _dot_nt(qh, kh) * d_intra
        num = _dot(qk.astype(BF16), vh) + s_inter * _dot(qh, c_prev.astype(BF16))
        den = jnp.sum(qk + s_inter * (qh.astype(F32) * n_prev), axis=1, keepdims=True)
        hh = num / jnp.maximum(jnp.abs(den), jnp.exp(-m_t))

        mu = jnp.mean(hh, axis=1, keepdims=True)
        hc = hh - mu
        var = jnp.mean(hc * hc, axis=1, keepdims=True)
        hn = hc * lax.rsqrt(var + LN_EPS) * hg_ref[:, hs]
        out_ref[0, :, hs] = (jax.nn.sigmoid(o_ref[0, :, hs].astype(F32)) * hn).astype(BF16)


def _mlstm_branch(q, k, kt, v, o, gates, gates_t, head_g):
    bsz, s, _ = q.shape
    nc = s // CHUNK
    blk = pl.BlockSpec((1, CHUNK, HALF), lambda bi, c: (bi, c, 0))
    return pl.pallas_call(
        _mlstm_kernel,
        grid=(bsz, nc),
        in_specs=[blk, blk, pl.BlockSpec((HALF, CHUNK), lambda bi, c: (0, bi * nc + c)), blk, blk,
                  pl.BlockSpec((1, CHUNK, LANES), lambda bi, c: (bi, c, 0)),
                  pl.BlockSpec((8, CHUNK), lambda bi, c: (0, bi * nc + c)),
                  _full_spec(head_g.shape)],
        out_specs=blk,
        out_shape=jax.ShapeDtypeStruct((bsz, s, HALF), BF16),
        scratch_shapes=[pltpu.VMEM((MLSTM_HEADS, MLSTM_HEAD_DIM, MLSTM_HEAD_DIM), F32),
                        pltpu.VMEM((MLSTM_HEADS, 8, MLSTM_HEAD_DIM), F32),
                        pltpu.VMEM((MLSTM_HEADS, 8, LANES), F32)],
        compiler_params=_params("parallel", "arbitrary"),
        name="mlstm_branch",
    )(q, k, kt, v, o, gates, gates_t, head_g)


def _mix_ln(a_ref, b_ref, x_ref, w_ref, g_ref, beta_ref):
    mix = _dot(a_ref[...], w_ref[0:HALF, :]) + _dot(b_ref[...], w_ref[HALF:, :])
    return _ln(ALPHA * x_ref[...] + mix, g_ref[...], beta_ref[...])


def _layer_spec(w, layer):
    zeros = (0,) * (w.ndim - 1)
    return pl.BlockSpec((None,) + w.shape[1:], lambda *_: (layer,) + zeros)


def _swiglu_into(xb_ref, w1_ref, w3_ref, w2_ref, acc_ref):
    for n, start in enumerate(range(0, D_FF, FF_CHUNK)):
        size = min(FF_CHUNK, D_FF - start)
        xb = xb_ref[...]
        h1 = _dot(xb, w1_ref[:, start:start + size])
        h3 = _dot(xb, w3_ref[:, start:start + size])
        hh = (h1 * jax.nn.sigmoid(h1) * h3).astype(BF16)
        out = _dot(hh, w2_ref[start:start + size, :])
        if n == 0:
            acc_ref[...] = out
        else:
            acc_ref[...] += out


def _mix_ffn_kernel(a_ref, b_ref, x_ref, wo_ref, g1_ref, beta1_ref, w1_ref, w3_ref, w2_ref, g2_ref, beta2_ref,
                    out_ref, x1_ref, xb_ref, acc_ref):
    x1_ref[...] = _mix_ln(a_ref, b_ref, x_ref, wo_ref, g1_ref, beta1_ref)
    xb_ref[...] = x1_ref[...].astype(BF16)
    _swiglu_into(xb_ref, w1_ref, w3_ref, w2_ref, acc_ref)
    out_ref[...] = _ln(ALPHA * x1_ref[...] + acc_ref[...], g2_ref[...], beta2_ref[...])


def _mix_ffn(a, b, x, wo, g1, beta1, w1, w3, w2, layer, g2, beta2, tm):
    t = x.shape[0]
    return pl.pallas_call(
        _mix_ffn_kernel,
        grid=(t // tm,),
        in_specs=[_row_spec(tm, HALF), _row_spec(tm, HALF), _row_spec(tm, D_MODEL),
                  _full_spec(wo.shape), _full_spec(g1.shape), _full_spec(beta1.shape),
                  _layer_spec(w1, layer), _layer_spec(w3, layer), _layer_spec(w2, layer),
                  _full_spec(g2.shape), _full_spec(beta2.shape)],
        out_specs=_row_spec(tm, D_MODEL),
        out_shape=jax.ShapeDtypeStruct((t, D_MODEL), F32),
        scratch_shapes=[pltpu.VMEM((tm, D_MODEL), F32), pltpu.VMEM((tm, D_MODEL), BF16),
                        pltpu.VMEM((tm, D_MODEL), F32)],
        compiler_params=_params("parallel"),
        name="mix_ln_dense_swiglu_ln",
    )(a, b, x, wo, g1, beta1, w1, w3, w2, g2, beta2)


def _odd_in_kernel(x_ref, w_ref, vg_ref, vb_ref, ws_ref, bs_ref, c_ref, q_ref, k_ref, v_ref):
    tm = x_ref.shape[0]
    xb = x_ref[...].astype(BF16)

    def proj(c):
        return _dot(xb, w_ref[:, c * HALF:(c + 1) * HALF])

    u = _gelu(proj(0))
    zn = _ln(_gelu(proj(1)), vg_ref[...], vb_ref[...]).astype(BF16)
    q_ref[...] = (proj(2) * SB_HEAD_DIM ** -0.5).astype(BF16)
    k_ref[...] = proj(3).astype(BF16)
    v_ref[...] = proj(4).astype(BF16)

    row = lax.broadcasted_iota(jnp.int32, (CHUNK, CHUNK), 0)
    col = lax.broadcasted_iota(jnp.int32, (CHUNK, CHUNK), 1)
    for g in range(GMLP_GROUPS):
        lanes = slice(g * LANES, (g + 1) * LANES)
        wc = jnp.where(col <= row, ws_ref[g], 0.0).astype(BF16)
        bias = bs_ref[:, g:g + 1]
        for r in range(tm // CHUNK):
            rows = slice(r * CHUNK, (r + 1) * CHUNK)
            sg = _dot(wc, zn[rows, lanes]) + bias
            c_ref[rows, lanes] = (u[rows, lanes] * sg).astype(BF16)


def _odd_in(x, w, vg, vb, ws, bs_t, tm):
    t = x.shape[0]
    half_bf = jax.ShapeDtypeStruct((t, HALF), BF16)
    return pl.pallas_call(
        _odd_in_kernel,
        grid=(t // tm,),
        in_specs=[_row_spec(tm, D_MODEL), _full_spec(w.shape), _full_spec(vg.shape), _full_spec(vb.shape),
                  _full_spec(ws.shape), _full_spec(bs_t.shape)],
        out_specs=[_row_spec(tm, HALF)] * 4,
        out_shape=[half_bf] * 4,
        compiler_params=_params("parallel"),
        name="odd_in_proj_gmlp",
    )(x, w, vg, vb, ws, bs_t)


def _sb_kernel(q_ref, k_ref, v_ref, out_ref, lsz_ref, sp_ref, run_ref, acc_ref):
    nb = q_ref.shape[0]
    tq = q_ref.shape[1]
    tk = SB_KEY_TILE
    ratio = tq // tk
    i = pl.program_id(2)
    lane = lax.broadcasted_iota(jnp.int32, (tq, LANES), 1)
    row = lax.broadcasted_iota(jnp.int32, (tk, tk), 0)
    col = lax.broadcasted_iota(jnp.int32, (tk, tk), 1)
    later = jnp.where(row > col, 1.0, 0.0).astype(BF16)
    qrow = lax.broadcasted_iota(jnp.int32, (tq, tk), 0)
    kcol = lax.broadcasted_iota(jnp.int32, (tq, tk), 1)
    last = ratio * (i + 1) - 1
    streams = []
    for b in range(nb):
        q = q_ref[b]
        zero = jnp.zeros_like(q)
        streams.append((b, 0, jnp.where(lane < SB_HEAD_DIM, q, zero)))
        streams.append((b, 1, jnp.where(lane >= SB_HEAD_DIM, q, zero)))

    def rows_of(k, diagonal):
        return slice((ratio - 1 - k) * tk, tq) if diagonal else slice(None)

    def scores(k, slot, diagonal):
        j = last - k
        rows = rows_of(k, diagonal)
        for b, h, qm in streams:
            kb = k_ref[b, pl.ds(pl.multiple_of(j * tk, tk), tk), :]
            z = _dot_nt(qm[rows], kb)
            sp = jnp.maximum(z, 0.0) + jnp.log(1.0 + jnp.exp2(jnp.abs(z) * -LOG2E))
            lsz = z - sp
            if diagonal:
                visible = (kcol + (ratio - 1 - k) * tk < qrow)[rows]
                sp = jnp.where(visible, sp, 0.0)
                lsz = jnp.where(visible, lsz, MASKED_SCORE)
            sp_ref[slot, b, h, rows] = sp.astype(BF16)
            run = run_ref[b, h, rows]
            for lt in range(tk // LANES):
                lanes = slice(lt * LANES, (lt + 1) * LANES)
                lsz_ref[slot, b, h, rows, lanes] = lsz[:, lanes] - run
            run_ref[b, h, rows] = run + jnp.sum(sp, axis=1, keepdims=True)

    def output(k, slot, diagonal=False):
        j = last - k
        rows = rows_of(k, diagonal)
        for b, h, _ in streams:
            vb = v_ref[b, pl.ds(pl.multiple_of(j * tk, tk), tk), :]
            att = jnp.exp2((lsz_ref[slot, b, h, rows] - _dot(sp_ref[slot, b, h, rows], later)) * LOG2E)
            acc_ref[b, h, rows] += _dot(att.astype(BF16), vb)

    run_ref[...] = jnp.zeros_like(run_ref)
    acc_ref[...] = jnp.zeros_like(acc_ref)
    scores(0, 0, True)
    for k in range(1, ratio):
        output(k - 1, (k - 1) % 2, True)
        scores(k, k % 2, True)

    def two_blocks(p, carry):
        output(2 * p - 1, 1)
        scores(2 * p, 0, False)
        output(2 * p, 0)
        scores(2 * p + 1, 1, False)
        return carry

    lax.fori_loop(ratio // 2, ratio * (i + 1) // 2, two_blocks, 0)
    output(last, 1)
    for b in range(nb):
        out_ref[b] = jnp.where(lane < SB_HEAD_DIM, acc_ref[b, 0], acc_ref[b, 1]).astype(BF16)


def _sb_attention(q, k, v, tq):
    bsz, s, _ = q.shape
    nb = SB_BATCH if bsz % SB_BATCH == 0 else 1
    kv_spec = pl.BlockSpec((nb, s, LANES), lambda bi, p, i: (bi, 0, p))
    q_spec = pl.BlockSpec((nb, tq, LANES), lambda bi, p, i: (bi, i, p))
    heads = LANES // SB_HEAD_DIM
    assert tq % (2 * SB_KEY_TILE) == 0, "the walk pairs key blocks, two scratch slots"
    return pl.pallas_call(
        _sb_kernel,
        grid=(bsz // nb, HALF // LANES, s // tq),
        in_specs=[q_spec, kv_spec, kv_spec],
        out_specs=q_spec,
        out_shape=jax.ShapeDtypeStruct((bsz, s, HALF), BF16),
        scratch_shapes=[pltpu.VMEM((2, nb, heads, tq, SB_KEY_TILE), F32),
                        pltpu.VMEM((2, nb, heads, tq, SB_KEY_TILE), BF16),
                        pltpu.VMEM((nb, heads, tq, LANES), F32), pltpu.VMEM((nb, heads, tq, LANES), F32)],
        compiler_params=_params("parallel", "parallel", "arbitrary"),
        name="stick_breaking_attention",
    )(q, k, v)


def _odd_out_kernel(c_ref, d_ref, x_ref, w_ref, g_ref, beta_ref, rw_ref, rb_ref,
                    x1_ref, gate_ref, route_ref, cnt_ref, base_ref):
    tm = x_ref.shape[0]

    @pl.when(pl.program_id(0) == 0)
    def _():
        base_ref[...] = jnp.zeros_like(base_ref)

    x1 = _mix_ln(c_ref, d_ref, x_ref, w_ref, g_ref, beta_ref)
    x1_ref[...] = x1

    xh, xl = _split_bf16(x1)
    logits = _dot(xh, rw_ref[0]) + _dot(xl, rw_ref[0]) + _dot(xh, rw_ref[1]) + rb_ref[...]
    lane = lax.broadcasted_iota(jnp.int32, (tm, LANES), 1)
    logits = jnp.where(lane < N_EXPERTS, logits, -jnp.inf)
    m1 = jnp.max(logits, axis=1, keepdims=True)
    i1 = jnp.min(jnp.where(logits == m1, lane, LANES), axis=1, keepdims=True)
    sel1 = lane == i1
    rest = jnp.where(sel1, -jnp.inf, logits)
    m2 = jnp.max(rest, axis=1, keepdims=True)
    i2 = jnp.min(jnp.where(rest == m2, lane, LANES), axis=1, keepdims=True)
    sel2 = lane == i2
    e = jnp.exp(m2 - m1)
    g1 = 1.0 / (1.0 + e)
    g2 = e / (1.0 + e)
    first_lower = i1 < i2
    gate_ref[...] = jnp.where(lane == 0, jnp.where(first_lower, g1, g2),
                              jnp.where(lane == 1, jnp.where(first_lower, g2, g1), 0.0))

    sel_f = jnp.where(sel1 | sel2, 1.0, 0.0)
    row = lax.broadcasted_iota(jnp.int32, (tm, tm), 0)
    col = lax.broadcasted_iota(jnp.int32, (tm, tm), 1)
    before = jnp.where(col < row, 1.0, 0.0).astype(BF16)
    rank = base_ref[0:1, :] + _dot(before, sel_f.astype(BF16))
    e_lo = jnp.minimum(i1, i2)
    e_hi = jnp.maximum(i1, i2)
    rank_lo = jnp.sum(jnp.where(lane == e_lo, rank, 0.0), axis=1, keepdims=True)
    rank_hi = jnp.sum(jnp.where(lane == e_hi, rank, 0.0), axis=1, keepdims=True)
    route_ref[...] = jnp.where(lane == 0, e_lo, jnp.where(lane == 1, e_hi, jnp.where(
        lane == 2, rank_lo.astype(jnp.int32), jnp.where(lane == 3, rank_hi.astype(jnp.int32), 0))))
    base_ref[...] = base_ref[...] + jnp.sum(sel_f, axis=0, keepdims=True)
    cnt_ref[...] = base_ref[...].astype(jnp.int32)


def _odd_out(c, d, x, w, g, beta, rw, rb, tm):
    t = x.shape[0]
    lane_spec = _row_spec(tm, LANES)
    return pl.pallas_call(
        _odd_out_kernel,
        grid=(t // tm,),
        in_specs=[_row_spec(tm, HALF), _row_spec(tm, HALF), _row_spec(tm, D_MODEL),
                  _full_spec(w.shape), _full_spec(g.shape), _full_spec(beta.shape),
                  _full_spec(rw.shape), _full_spec(rb.shape)],
        out_specs=[_row_spec(tm, D_MODEL), lane_spec, lane_spec, _full_spec((8, LANES))],
        out_shape=[jax.ShapeDtypeStruct((t, D_MODEL), F32), jax.ShapeDtypeStruct((t, LANES), F32),
                   jax.ShapeDtypeStruct((t, LANES), jnp.int32), jax.ShapeDtypeStruct((8, LANES), jnp.int32)],
        scratch_shapes=[pltpu.VMEM((8, LANES), F32)],
        compiler_params=_params("arbitrary"),
        name="odd_out_proj_ln_router",
    )(c, d, x, w, g, beta, rw, rb)


def _for_rows(n, fn):
    def body(r, carry):
        fn(r)
        return carry
    lax.fori_loop(0, n, body, 0, unroll=8)


def _dispatch_kernel(pos_ref, x_ref, xs_hbm, slab_ref, sem):
    tm = x_ref.shape[0]
    for s in range(SLAB):
        slab_ref[pl.ds(s, tm, stride=SLAB), :] = x_ref[:, s * LANES:(s + 1) * LANES]

    def start(r):
        src = slab_ref.at[pl.ds(pl.multiple_of(r * SLAB, SLAB), SLAB)]
        for k in range(2):
            dst = pl.multiple_of(pos_ref[0, 0, k * tm + r] * SLAB, SLAB)
            pltpu.make_async_copy(src, xs_hbm.at[pl.ds(dst, SLAB)], sem).start(priority=k)

    _for_rows(tm, start)
    for _ in range(2):
        pltpu.make_async_copy(slab_ref, xs_hbm.at[pl.ds(0, tm * SLAB)], sem).wait()


def _dispatch(x1, pos, tm):
    t = x1.shape[0]
    return pl.pallas_call(
        _dispatch_kernel,
        grid=(t // tm,),
        in_specs=[pl.BlockSpec((1, 1, 2 * tm), lambda i: (i, 0, 0), memory_space=pltpu.SMEM),
                  _row_spec(tm, D_MODEL)],
        out_specs=pl.BlockSpec(memory_space=pl.ANY),
        out_shape=jax.ShapeDtypeStruct((2 * t * SLAB, LANES), F32),
        scratch_shapes=[pltpu.VMEM((tm * SLAB, LANES), F32), pltpu.SemaphoreType.DMA(())],
        compiler_params=_params("arbitrary"),
        name="moe_dispatch",
    )(pos, x1)


def _moe_kernel(tile_ref, expert_ref, lo_ref, hi_ref, xs_ref, w1_ref, w3_ref, w2_ref, y_ref, xb_ref, acc_ref):
    tm = xb_ref.shape[0]
    step = pl.program_id(0)
    lo = lo_ref[step]
    hi = hi_ref[step]

    @pl.when(hi > lo)
    def _():
        for s in range(SLAB):
            xb_ref[:, s * LANES:(s + 1) * LANES] = xs_ref[pl.ds(s, tm, stride=SLAB), :].astype(BF16)
        _swiglu_into(xb_ref, w1_ref, w3_ref, w2_ref, acc_ref)

        @pl.when(lo == 0)
        def _():
            for s in range(SLAB):
                y_ref[pl.ds(s, tm, stride=SLAB), :] = acc_ref[:, s * LANES:(s + 1) * LANES]

        @pl.when(lo > 0)
        def _():
            row = lax.broadcasted_iota(jnp.int32, (tm, LANES), 0)
            for s in range(SLAB):
                rows = pl.ds(s, tm, stride=SLAB)
                y_ref[rows, :] = jnp.where(row >= lo, acc_ref[:, s * LANES:(s + 1) * LANES], y_ref[rows, :])


def _moe(xs, step_tile, step_expert, step_lo, step_hi, w1, w3, w2, layer, tm):
    n_steps = step_tile.shape[0]
    w_spec = lambda shape: pl.BlockSpec((None, None) + shape[2:], lambda s, tile, ex, lo, hi: (layer, ex[s], 0, 0))
    row_spec = pl.BlockSpec((tm * SLAB, LANES), lambda s, tile, ex, lo, hi: (tile[s], 0))
    grid_spec = pltpu.PrefetchScalarGridSpec(
        num_scalar_prefetch=4,
        grid=(n_steps,),
        in_specs=[row_spec, w_spec(w1.shape), w_spec(w3.shape), w_spec(w2.shape)],
        out_specs=row_spec,
        scratch_shapes=[pltpu.VMEM((tm, D_MODEL), BF16), pltpu.VMEM((tm, D_MODEL), F32)],
    )
    return pl.pallas_call(
        _moe_kernel,
        grid_spec=grid_spec,
        out_shape=jax.ShapeDtypeStruct(xs.shape, F32),
        compiler_params=_params("arbitrary"),
        name="moe_grouped_swiglu",
    )(step_tile, step_expert, step_lo, step_hi, xs, w1, w3, w2)


def _combine_kernel(pos_ref, pos_next_ref, x_ref, gate_ref, g_ref, beta_ref, y_hbm, out_ref, ybuf, ffn_ref, sem):
    tm = x_ref.shape[0]
    i = pl.program_id(0)
    slot = i % 2

    def start_tile(p_ref, buf):
        def start(r):
            for k in range(2):
                src = pl.multiple_of(p_ref[0, 0, k * tm + r] * SLAB, SLAB)
                pltpu.make_async_copy(y_hbm.at[pl.ds(src, SLAB)],
                                      ybuf.at[buf, k, pl.ds(pl.multiple_of(r * SLAB, SLAB), SLAB)],
                                      sem.at[buf]).start(priority=k)
        _for_rows(tm, start)

    @pl.when(i == 0)
    def _():
        start_tile(pos_ref, 0)

    @pl.when(i + 1 < pl.num_programs(0))
    def _():
        start_tile(pos_next_ref, 1 - slot)

    for k in range(2):
        pltpu.make_async_copy(y_hbm.at[pl.ds(0, tm * SLAB)], ybuf.at[slot, k], sem.at[slot]).wait()
    gate = gate_ref[...]
    for s in range(SLAB):
        rows = pl.ds(s, tm, stride=SLAB)
        ffn_ref[:, s * LANES:(s + 1) * LANES] = (gate[:, 0:1] * ybuf[slot, 0, rows, :]
                                                 + gate[:, 1:2] * ybuf[slot, 1, rows, :])
    out_ref[...] = _ln(ALPHA * x_ref[...] + ffn_ref[...], g_ref[...], beta_ref[...])


def _combine(x1, y, pos, gate, g, beta, tm):
    t = x1.shape[0]
    nt = t // tm
    pos_spec = lambda f: pl.BlockSpec((1, 1, 2 * tm), f, memory_space=pltpu.SMEM)
    return pl.pallas_call(
        _combine_kernel,
        grid=(nt,),
        in_specs=[pos_spec(lambda i: (i, 0, 0)), pos_spec(lambda i: (jnp.minimum(i + 1, nt - 1), 0, 0)),
                  _row_spec(tm, D_MODEL), _row_spec(tm, LANES), _full_spec(g.shape), _full_spec(beta.shape),
                  pl.BlockSpec(memory_space=pl.ANY)],
        out_specs=_row_spec(tm, D_MODEL),
        out_shape=jax.ShapeDtypeStruct((t, D_MODEL), F32),
        scratch_shapes=[pltpu.VMEM((2, 2, tm * SLAB, LANES), F32), pltpu.VMEM((tm, D_MODEL), F32),
                        pltpu.SemaphoreType.DMA((2,))],
        compiler_params=_params("arbitrary"),
        name="moe_combine_ln",
    )(pos, pos, x1, gate, g, beta, y)


def _routing_tables(route, cnt, t, tm):
    counts = cnt[0, :N_EXPERTS]
    ends = jnp.cumsum(counts)
    starts = ends - counts
    experts = jnp.arange(N_EXPERTS, dtype=jnp.int32)

    def sorted_row(e, rank):
        return jnp.sum(jnp.where(e[:, None] == experts[None, :], starts[None, :], 0), axis=1) + rank

    pos = jnp.stack([sorted_row(route[:, 0], route[:, 2]), sorted_row(route[:, 1], route[:, 3])])
    nt = t // tm
    pos = pos.reshape(2, nt, tm).transpose(1, 0, 2).reshape(nt, 1, 2 * tm)

    n_steps = (2 * t) // tm + N_EXPERTS - 1
    first_tile = starts // tm
    n_visits = jnp.where(counts > 0, (ends - 1) // tm - first_tile + 1, 0)
    step_ends = jnp.cumsum(n_visits)
    total = step_ends[-1]
    s = jnp.minimum(jnp.arange(n_steps, dtype=jnp.int32), total - 1)
    step_expert = jnp.sum((s[:, None] >= step_ends[None, :]).astype(jnp.int32), axis=1)
    step_tile = first_tile[step_expert] + s - (step_ends - n_visits)[step_expert]
    lo = jnp.clip(starts[step_expert] - step_tile * tm, 0, tm)
    hi = jnp.clip(ends[step_expert] - step_tile * tm, 0, tm)
    hi = jnp.where(jnp.arange(n_steps) < total, hi, lo)
    as_i32 = lambda a: a.astype(jnp.int32)
    return pos.astype(jnp.int32), as_i32(step_tile), as_i32(step_expert), as_i32(lo), as_i32(hi)


def _row(v):
    return v.reshape(1, -1).astype(F32)


def _even_layer(x, bsz, s, w_in, gate_bias, conv_w, conv_b, norm_g, norm_b, head_g, w_out,
                ln1_g, ln1_b, w1, w3, w2, layer, ln2_g, ln2_b):
    main = 6 * HALF
    w_main = w_in[:, :main].astype(BF16)
    w_gate = jnp.pad(w_in[:, main:], ((0, 0), (0, LANES - 2 * MLSTM_HEADS))).astype(BF16)
    gb = jnp.pad(gate_bias, (0, LANES - 2 * MLSTM_HEADS)).reshape(1, LANES)
    glu, q, k, v, o, gates, kt, gates_t = _even_in(x, w_main, w_gate, gb, TOKEN_TILE)
    shp = lambda a: a.reshape(bsz, s, a.shape[-1])
    a = _conv_branch(shp(glu), conv_w, conv_b, _row(norm_g), _row(norm_b), min(s, 512))
    hb = _mlstm_branch(shp(q), shp(k), kt, shp(v), shp(o), shp(gates), gates_t, _row(head_g))
    return _mix_ffn(a.reshape(-1, HALF), hb.reshape(-1, HALF), x, w_out.astype(BF16), _row(ln1_g), _row(ln1_b),
                    w1, w3, w2, layer, _row(ln2_g), _row(ln2_b), TOKEN_TILE)


def _odd_layer(x, bsz, s, w_in, v_g, v_b, w_s, b_s, w_out, ln1_g, ln1_b, router_w, router_b,
               w1, w3, w2, layer, ln2_g, ln2_b):
    t = x.shape[0]
    c, q, k, v = _odd_in(x, w_in.astype(BF16), _row(v_g), _row(v_b), w_s, b_s.T, TOKEN_TILE)
    shp = lambda a: a.reshape(bsz, s, HALF)
    d = _sb_attention(shp(q), shp(k), shp(v), min(s, SB_TILE)).reshape(t, HALF)
    rw = jnp.pad(router_w, ((0, 0), (0, LANES - N_EXPERTS)))
    rw_hi = rw.astype(BF16)
    rw_lo = (rw - rw_hi.astype(F32)).astype(BF16)
    rb = jnp.pad(router_b, (0, LANES - N_EXPERTS)).reshape(1, LANES)
    x1, gate, route, cnt = _odd_out(c, d, x, w_out.astype(BF16), _row(ln1_g), _row(ln1_b),
                                    jnp.stack([rw_hi, rw_lo]), rb, TOKEN_TILE)
    pos, step_tile, step_expert, step_lo, step_hi = _routing_tables(route, cnt, t, MOE_TILE)
    xs = _dispatch(x1, pos, MOE_TILE)
    y = _moe(xs, step_tile, step_expert, step_lo, step_hi, w1, w3, w2, layer, MOE_TILE)
    return _combine(x1, y, pos, gate, _row(ln2_g), _row(ln2_b), MOE_TILE)


def kernel(x, ab_w_in, ab_gate_bias, a_conv_w, a_conv_b, a_norm_g, a_norm_b, b_norm_g, ab_w_out, ab_ln1_g, ab_ln1_b, ffn_w1, ffn_w3, ffn_w2, ab_ln2_g, ab_ln2_b, cd_w_in, c_norm_g, c_norm_b, c_w_s, c_b_s, cd_w_out, cd_ln1_g, cd_ln1_b, router_w, router_b, moe_w1, moe_w3, moe_w2, cd_ln2_g, cd_ln2_b):
    bsz, s, _ = x.shape
    h = x.reshape(bsz * s, D_MODEL)
    ffn_w = [w.astype(BF16) for w in (ffn_w1, ffn_w3, ffn_w2)]
    moe_w = [w.astype(BF16) for w in (moe_w1, moe_w3, moe_w2)]
    for layer in range(DEPTH):
        j = layer // 2
        if layer % 2 == 0:
            h = _even_layer(h, bsz, s, ab_w_in[j], ab_gate_bias[j], a_conv_w[j], a_conv_b[j], a_norm_g[j],
                            a_norm_b[j], b_norm_g[j], ab_w_out[j], ab_ln1_g[j], ab_ln1_b[j],
                            *ffn_w, j, ab_ln2_g[j], ab_ln2_b[j])
        else:
            h = _odd_layer(h, bsz, s, cd_w_in[j], c_norm_g[j], c_norm_b[j], c_w_s[j], c_b_s[j], cd_w_out[j],
                           cd_ln1_g[j], cd_ln1_b[j], router_w[j], router_b[j],
                           *moe_w, j, cd_ln2_g[j], cd_ln2_b[j])
    return h.reshape(bsz, s, D_MODEL)
```

```python
import jax
import jax.numpy as jnp
from jax import lax
from jax.experimental import pallas as pl
from jax.experimental.pallas import tpu as pltpu

F32 = jnp.float32
BF16 = jnp.bfloat16

D_MODEL = 1024
DEPTH = 4
HALF = D_MODEL // 2
CONV_WIDTH = 31
CONV_ROWS = 16
CONV_HALO = 32
MLSTM_HEADS = 4
MLSTM_HEAD_DIM = HALF // MLSTM_HEADS
CHUNK = 128
GMLP_GROUPS = 4
SB_HEADS = 8
SB_HEAD_DIM = HALF // SB_HEADS
D_FF = 2816
FF_CHUNK = 512
N_EXPERTS = 8
ALPHA = (2 * DEPTH) ** 0.25
LN_EPS = 1e-5
LOG2E = 1.4426950408889634
MASKED_SCORE = -1e30
LANES = 128
SLAB = D_MODEL // LANES
V7X_VMEM_LIMIT = 56 * 1024 * 1024

PROJ_TILE = 1024
TOKEN_TILE = 512
MOE_TILE = 512
SB_TILE = 1024
SB_BATCH = 2
SB_KEY_TILE = 256


def _ln(x, g, b):
    mu = jnp.mean(x, axis=-1, keepdims=True)
    xc = x - mu
    var = jnp.mean(xc * xc, axis=-1, keepdims=True)
    return xc * lax.rsqrt(var + LN_EPS) * g + b


def _gelu(x):
    return 0.5 * x * (1.0 + lax.erf(x * (0.5 ** 0.5)))


def _dot(a, b):
    return jnp.dot(a, b, preferred_element_type=F32)


def _dot_nt(a, b):
    return lax.dot_general(a, b, (((1,), (1,)), ((), ())), preferred_element_type=F32)


def _split_bf16(x):
    hi = x.astype(BF16)
    lo = (x - hi.astype(F32)).astype(BF16)
    return hi, lo


def _params(*sem):
    return pltpu.CompilerParams(dimension_semantics=sem, vmem_limit_bytes=V7X_VMEM_LIMIT)


def _row_spec(tm, width):
    return pl.BlockSpec((tm, width), lambda i: (i, 0))


def _full_spec(shape):
    zeros = (0,) * len(shape)
    return pl.BlockSpec(shape, lambda *_: zeros)


def _even_in_kernel(x_ref, w_ref, wg_ref, gb_ref, glu_ref, q_ref, k_ref, v_ref, o_ref, g_ref, kt_ref, gt_ref):
    xb = x_ref[...].astype(BF16)

    def proj(c):
        return _dot(xb, w_ref[:, c * HALF:(c + 1) * HALF])

    glu_ref[...] = proj(0) * jax.nn.sigmoid(proj(1))
    q_ref[...] = proj(2).astype(BF16)
    k = proj(3) * MLSTM_HEAD_DIM ** -0.5
    k_ref[...] = k.astype(BF16)
    kt_ref[...] = k.T.astype(BF16)
    v_ref[...] = proj(4).astype(BF16)
    o_ref[...] = proj(5).astype(BF16)
    gates = _dot(xb, wg_ref[...]) + gb_ref[...]
    g_ref[...] = gates
    gt_ref[...] = gates.T[0:8, :]


def _even_in(x, w, wg, gb, tm):
    t = x.shape[0]
    half_bf = jax.ShapeDtypeStruct((t, HALF), BF16)
    return pl.pallas_call(
        _even_in_kernel,
        grid=(t // tm,),
        in_specs=[_row_spec(tm, D_MODEL), _full_spec(w.shape), _full_spec(wg.shape), _full_spec(gb.shape)],
        out_specs=[_row_spec(tm, HALF)] * 5 + [_row_spec(tm, LANES),
                   pl.BlockSpec((HALF, tm), lambda i: (0, i)), pl.BlockSpec((8, tm), lambda i: (0, i))],
        out_shape=[jax.ShapeDtypeStruct((t, HALF), F32), half_bf, half_bf, half_bf, half_bf,
                   jax.ShapeDtypeStruct((t, LANES), F32),
                   jax.ShapeDtypeStruct((HALF, t), BF16), jax.ShapeDtypeStruct((8, t), F32)],
        compiler_params=_params("parallel"),
        name="even_in_proj",
    )(x, w, wg, gb)


def _conv_kernel(prev_ref, cur_ref, w_ref, cb_ref, g_ref, b_ref, out_ref, win_ref, y_ref):
    nseq = cur_ref.shape[0]
    ts = cur_ref.shape[1]
    nblk = HALF // LANES
    first = pl.program_id(1) == 0
    for b in range(nseq):
        for c in range(nblk):
            lanes = slice(c * LANES, (c + 1) * LANES)
            sub = b * nblk + c
            win_ref[pl.ds(sub, CONV_HALO, stride=SLAB), :] = jnp.where(first, 0.0, prev_ref[b, :, lanes])
            win_ref[pl.ds(CONV_HALO * SLAB + sub, ts, stride=SLAB), :] = cur_ref[b, :, lanes]
    lead = CONV_HALO - (CONV_WIDTH - 1)
    rows = CONV_ROWS * SLAB

    def body(n, carry):
        base = pl.multiple_of(n * rows, rows)
        acc = jnp.tile(cb_ref[...], (CONV_ROWS, 1))
        for j in range(CONV_WIDTH):
            acc = acc + jnp.tile(w_ref[j], (CONV_ROWS, 1)) * win_ref[pl.ds(base + (lead + j) * SLAB, rows), :]
        y_ref[pl.ds(base, rows), :] = acc
        return carry

    lax.fori_loop(0, ts // CONV_ROWS, body, 0)
    for b in range(nseq):
        y = jnp.concatenate([y_ref[pl.ds(b * nblk + c, ts, stride=SLAB), :] for c in range(nblk)], axis=1)
        yn = _ln(y, g_ref[...], b_ref[...])
        out_ref[b] = (yn * jax.nn.sigmoid(yn)).astype(BF16)


def _conv_branch(glu, w, cb, g, b, ts):
    bsz, s, _ = glu.shape
    nseq = SLAB * LANES // HALF
    assert bsz % nseq == 0
    per = ts // CONV_HALO
    w_tiles = jnp.tile(jnp.pad(w, ((0, 32 - CONV_WIDTH), (0, 0))).reshape(32, HALF // LANES, LANES), (1, nseq, 1))
    cb_tile = jnp.tile(cb.reshape(HALF // LANES, LANES), (nseq, 1))
    return pl.pallas_call(
        _conv_kernel,
        grid=(bsz // nseq, s // ts),
        in_specs=[
            pl.BlockSpec((nseq, CONV_HALO, HALF), lambda bi, i: (bi, jnp.maximum(i * per - 1, 0), 0)),
            pl.BlockSpec((nseq, ts, HALF), lambda bi, i: (bi, i, 0)),
            _full_spec(w_tiles.shape), _full_spec(cb_tile.shape), _full_spec(g.shape), _full_spec(b.shape),
        ],
        out_specs=pl.BlockSpec((nseq, ts, HALF), lambda bi, i: (bi, i, 0)),
        out_shape=jax.ShapeDtypeStruct((bsz, s, HALF), BF16),
        scratch_shapes=[pltpu.VMEM(((ts + CONV_HALO) * SLAB, LANES), F32), pltpu.VMEM((ts * SLAB, LANES), F32)],
        compiler_params=_params("parallel", "parallel"),
        name="conv_branch",
    )(glu, glu, w_tiles, cb_tile, g, b)


def _mlstm_kernel(q_ref, k_ref, kt_ref, v_ref, o_ref, g_ref, gt_ref, hg_ref, out_ref, c_ref, n_ref, m_ref):
    L, H, d = CHUNK, MLSTM_HEADS, MLSTM_HEAD_DIM

    @pl.when(pl.program_id(1) == 0)
    def _():
        c_ref[...] = jnp.zeros_like(c_ref)
        n_ref[...] = jnp.zeros_like(n_ref)
        m_ref[...] = jnp.zeros_like(m_ref)

    row = lax.broadcasted_iota(jnp.int32, (L, L), 0)
    col = lax.broadcasted_iota(jnp.int32, (L, L), 1)
    causal = col <= row
    gates = g_ref[0]
    lf_hi, lf_lo = _split_bf16(jax.nn.log_sigmoid(gates))
    lower = jnp.where(causal, 1.0, 0.0).astype(BF16)
    bcum = _dot(lower, lf_hi) + _dot(lower, lf_lo)
    gates_r = gt_ref[...]
    lr_hi, lr_lo = _split_bf16(jax.nn.log_sigmoid(gates_r))
    upper = jnp.where(row <= col, 1.0, 0.0).astype(BF16)
    bcum_r = _dot(lr_hi, upper) + _dot(lr_lo, upper)

    for h in range(H):
        hs = slice(h * d, (h + 1) * d)
        qh, kh, vh = q_ref[0, :, hs], k_ref[0, :, hs], v_ref[0, :, hs]
        b_c = bcum[:, H + h:H + h + 1]
        i_r = gates_r[h:h + 1, :]
        b_r = bcum_r[H + h:H + h + 1, :]
        gtot = bcum[L - 1:L, H + h:H + h + 1]

        c_prev = c_ref[h]
        n_prev = n_ref[h, 0:1, :]
        m_prev = m_ref[h, 0:1, 0:1]

        log_w = gtot - b_r + i_r
        a = jnp.max(log_w, axis=1, keepdims=True)
        w = jnp.exp(log_w - a)
        kw_t = (kt_ref[hs, :].astype(F32) * w).astype(BF16)
        kv_loc = _dot(kw_t, vh)
        n_loc = _dot_nt(jnp.broadcast_to(w, (8, L)).astype(BF16), kt_ref[hs, :])[0:1, :]

        m_new = jnp.maximum(gtot + m_prev, a)
        s_old = jnp.exp(gtot + m_prev - m_new)
        s_new = jnp.exp(a - m_new)
        c_ref[h] = s_old * c_prev + s_new * kv_loc
        n_ref[h] = jnp.broadcast_to(s_old * n_prev + s_new * n_loc, (8, d))
        m_ref[h] = jnp.broadcast_to(m_new, (8, LANES))

        log_d = jnp.where(causal, b_c - b_r + i_r, -jnp.inf)
        log_inter = b_c + m_prev
        m_t = jnp.maximum(log_inter, jnp.max(log_d, axis=1, keepdims=True))
        d_intra = jnp.exp(log_d - m_t)
        s_inter = jnp.exp(log_inter - m_t)
        qk = _dot_nt(qh, kh) * d_intra
        num = _dot(qk.astype(BF16), vh) + s_inter * _dot(qh, c_prev.astype(BF16))
        den = jnp.sum(qk + s_inter * (qh.astype(F32) * n_prev), axis=1, keepdims=True)
        hh = num / jnp.maximum(jnp.abs(den), jnp.exp(-m_t))

        mu = jnp.mean(hh, axis=1, keepdims=True)
        hc = hh - mu
        var = jnp.mean(hc * hc, axis=1, keepdims=True)
        hn = hc * lax.rsqrt(var + LN_EPS) * hg_ref[:, hs]
        out_ref[0, :, hs] = (jax.nn.sigmoid(o_ref[0, :, hs].astype(F32)) * hn).astype(BF16)


def _mlstm_branch(q, k, kt, v, o, gates, gates_t, head_g):
    bsz, s, _ = q.shape
    nc = s // CHUNK
    blk = pl.BlockSpec((1, CHUNK, HALF), lambda bi, c: (bi, c, 0))
    return pl.pallas_call(
        _mlstm_kernel,
        grid=(bsz, nc),
        in_specs=[blk, blk, pl.BlockSpec((HALF, CHUNK), lambda bi, c: (0, bi * nc + c)), blk, blk,
                  pl.BlockSpec((1, CHUNK, LANES), lambda bi, c: (bi, c, 0)),
                  pl.BlockSpec((8, CHUNK), lambda bi, c: (0, bi * nc + c)),
                  _full_spec(head_g.shape)],
        out_specs=blk,
        out_shape=jax.ShapeDtypeStruct((bsz, s, HALF), BF16),
        scratch_shapes=[pltpu.VMEM((MLSTM_HEADS, MLSTM_HEAD_DIM, MLSTM_HEAD_DIM), F32),
                        pltpu.VMEM((MLSTM_HEADS, 8, MLSTM_HEAD_DIM), F32),
                        pltpu.VMEM((MLSTM_HEADS, 8, LANES), F32)],
        compiler_params=_params("parallel", "arbitrary"),
        name="mlstm_branch",
    )(q, k, kt, v, o, gates, gates_t, head_g)


def _mix_ln(a_ref, b_ref, x_ref, w_ref, g_ref, beta_ref):
    mix = _dot(a_ref[...], w_ref[0:HALF, :]) + _dot(b_ref[...], w_ref[HALF:, :])
    return _ln(ALPHA * x_ref[...] + mix, g_ref[...], beta_ref[...])


def _layer_spec(w, layer):
    zeros = (0,) * (w.ndim - 1)
    return pl.BlockSpec((None,) + w.shape[1:], lambda *_: (layer,) + zeros)


def _swiglu_into(xb_ref, w1_ref, w3_ref, w2_ref, acc_ref):
    for n, start in enumerate(range(0, D_FF, FF_CHUNK)):
        size = min(FF_CHUNK, D_FF - start)
        xb = xb_ref[...]
        h1 = _dot(xb, w1_ref[:, start:start + size])
        h3 = _dot(xb, w3_ref[:, start:start + size])
        hh = (h1 * jax.nn.sigmoid(h1) * h3).astype(BF16)
        out = _dot(hh, w2_ref[start:start + size, :])
        if n == 0:
            acc_ref[...] = out
        else:
            acc_ref[...] += out


def _mix_ffn_kernel(a_ref, b_ref, x_ref, wo_ref, g1_ref, beta1_ref, w1_ref, w3_ref, w2_ref, g2_ref, beta2_ref,
                    out_ref, x1_ref, xb_ref, acc_ref):
    x1_ref[...] = _mix_ln(a_ref, b_ref, x_ref, wo_ref, g1_ref, beta1_ref)
    xb_ref[...] = x1_ref[...].astype(BF16)
    _swiglu_into(xb_ref, w1_ref, w3_ref, w2_ref, acc_ref)
    out_ref[...] = _ln(ALPHA * x1_ref[...] + acc_ref[...], g2_ref[...], beta2_ref[...])


def _mix_ffn(a, b, x, wo, g1, beta1, w1, w3, w2, layer, g2, beta2, tm):
    t = x.shape[0]
    return pl.pallas_call(
        _mix_ffn_kernel,
        grid=(t // tm,),
        in_specs=[_row_spec(tm, HALF), _row_spec(tm, HALF), _row_spec(tm, D_MODEL),
                  _full_spec(wo.shape), _full_spec(g1.shape), _full_spec(beta1.shape),
                  _layer_spec(w1, layer), _layer_spec(w3, layer), _layer_spec(w2, layer),
                  _full_spec(g2.shape), _full_spec(beta2.shape)],
        out_specs=_row_spec(tm, D_MODEL),
        out_shape=jax.ShapeDtypeStruct((t, D_MODEL), F32),
        scratch_shapes=[pltpu.VMEM((tm, D_MODEL), F32), pltpu.VMEM((tm, D_MODEL), BF16),
                        pltpu.VMEM((tm, D_MODEL), F32)],
        compiler_params=_params("parallel"),
        name="mix_ln_dense_swiglu_ln",
    )(a, b, x, wo, g1, beta1, w1, w3, w2, g2, beta2)


def _odd_in_kernel(x_ref, w_ref, vg_ref, vb_ref, ws_ref, bs_ref, c_ref, q_ref, k_ref, v_ref):
    tm = x_ref.shape[0]
    xb = x_ref[...].astype(BF16)

    def proj(c):
        return _dot(xb, w_ref[:, c * HALF:(c + 1) * HALF])

    u = _gelu(proj(0))
    zn = _ln(_gelu(proj(1)), vg_ref[...], vb_ref[...]).astype(BF16)
    q_ref[...] = (proj(2) * SB_HEAD_DIM ** -0.5).astype(BF16)
    k_ref[...] = proj(3).astype(BF16)
    v_ref[...] = proj(4).astype(BF16)

    row = lax.broadcasted_iota(jnp.int32, (CHUNK, CHUNK), 0)
    col = lax.broadcasted_iota(jnp.int32, (CHUNK, CHUNK), 1)
    for g in range(GMLP_GROUPS):
        lanes = slice(g * LANES, (g + 1) * LANES)
        wc = jnp.where(col <= row, ws_ref[g], 0.0).astype(BF16)
        bias = bs_ref[:, g:g + 1]
        for r in range(tm // CHUNK):
            rows = slice(r * CHUNK, (r + 1) * CHUNK)
            sg = _dot(wc, zn[rows, lanes]) + bias
            c_ref[rows, lanes] = (u[rows, lanes] * sg).astype(BF16)


def _odd_in(x, w, vg, vb, ws, bs_t, tm):
    t = x.shape[0]
    half_bf = jax.ShapeDtypeStruct((t, HALF), BF16)
    return pl.pallas_call(
        _odd_in_kernel,
        grid=(t // tm,),
        in_specs=[_row_spec(tm, D_MODEL), _full_spec(w.shape), _full_spec(vg.shape), _full_spec(vb.shape),
                  _full_spec(ws.shape), _full_spec(bs_t.shape)],
        out_specs=[_row_spec(tm, HALF)] * 4,
        out_shape=[half_bf] * 4,
        compiler_params=_params("parallel"),
        name="odd_in_proj_gmlp",
    )(x, w, vg, vb, ws, bs_t)


def _sb_kernel(q_ref, k_ref, v_ref, out_ref, lsz_ref, sp_ref, run_ref, acc_ref):
    nb = q_ref.shape[0]
    tq = q_ref.shape[1]
    tk = SB_KEY_TILE
    ratio = tq // tk
    i = pl.program_id(2)
    lane = lax.broadcasted_iota(jnp.int32, (tq, LANES), 1)
    row = lax.broadcasted_iota(jnp.int32, (tk, tk), 0)
    col = lax.broadcasted_iota(jnp.int32, (tk, tk), 1)
    later = jnp.where(row > col, 1.0, 0.0).astype(BF16)
    qrow = lax.broadcasted_iota(jnp.int32, (tq, tk), 0)
    kcol = lax.broadcasted_iota(jnp.int32, (tq, tk), 1)
    last = ratio * (i + 1) - 1
    streams = []
    for b in range(nb):
        q = q_ref[b]
        zero = jnp.zeros_like(q)
        streams.append((b, 0, jnp.where(lane < SB_HEAD_DIM, q, zero)))
        streams.append((b, 1, jnp.where(lane >= SB_HEAD_DIM, q, zero)))

    def rows_of(k, diagonal):
        return slice((ratio - 1 - k) * tk, tq) if diagonal else slice(None)

    def scores(k, slot, diagonal):
        j = last - k
        rows = rows_of(k, diagonal)
        for b, h, qm in streams:
            kb = k_ref[b, pl.ds(pl.multiple_of(j * tk, tk), tk), :]
            z = _dot_nt(qm[rows], kb)
            sp = jnp.maximum(z, 0.0) + jnp.log(1.0 + jnp.exp2(jnp.abs(z) * -LOG2E))
            lsz = z - sp
            if diagonal:
                visible = (kcol + (ratio - 1 - k) * tk < qrow)[rows]
                sp = jnp.where(visible, sp, 0.0)
                lsz = jnp.where(visible, lsz, MASKED_SCORE)
            sp_ref[slot, b, h, rows] = sp.astype(BF16)
            run = run_ref[b, h, rows]
            for lt in range(tk // LANES):
                lanes = slice(lt * LANES, (lt + 1) * LANES)
                lsz_ref[slot, b, h, rows, lanes] = lsz[:, lanes] - run
            run_ref[b, h, rows] = run + jnp.sum(sp, axis=1, keepdims=True)

    def output(k, slot, diagonal=False):
        j = last - k
        rows = rows_of(k, diagonal)
        for b, h, _ in streams:
            vb = v_ref[b, pl.ds(pl.multiple_of(j * tk, tk), tk), :]
            att = jnp.exp2((lsz_ref[slot, b, h, rows] - _dot(sp_ref[slot, b, h, rows], later)) * LOG2E)
            acc_ref[b, h, rows] += _dot(att.astype(BF16), vb)

    run_ref[...] = jnp.zeros_like(run_ref)
    acc_ref[...] = jnp.zeros_like(acc_ref)
    scores(0, 0, True)
    for k in range(1, ratio):
        output(k - 1, (k - 1) % 2, True)
        scores(k, k % 2, True)

    def two_blocks(p, carry):
        output(2 * p - 1, 1)
        scores(2 * p, 0, False)
        output(2 * p, 0)
        scores(2 * p + 1, 1, False)
        return carry

    lax.fori_loop(ratio // 2, ratio * (i + 1) // 2, two_blocks, 0)
    output(last, 1)
    for b in range(nb):
        out_ref[b] = jnp.where(lane < SB_HEAD_DIM, acc_ref[b, 0], acc_ref[b, 1]).astype(BF16)


def _sb_attention(q, k, v, tq):
    bsz, s, _ = q.shape
    nb = SB_BATCH if bsz % SB_BATCH == 0 else 1
    kv_spec = pl.BlockSpec((nb, s, LANES), lambda bi, p, i: (bi, 0, p))
    q_spec = pl.BlockSpec((nb, tq, LANES), lambda bi, p, i: (bi, i, p))
    heads = LANES // SB_HEAD_DIM
    assert tq % (2 * SB_KEY_TILE) == 0, "the walk pairs key blocks, two scratch slots"
    return pl.pallas_call(
        _sb_kernel,
        grid=(bsz // nb, HALF // LANES, s // tq),
        in_specs=[q_spec, kv_spec, kv_spec],
        out_specs=q_spec,
        out_shape=jax.ShapeDtypeStruct((bsz, s, HALF), BF16),
        scratch_shapes=[pltpu.VMEM((2, nb, heads, tq, SB_KEY_TILE), F32),
                        pltpu.VMEM((2, nb, heads, tq, SB_KEY_TILE), BF16),
                        pltpu.VMEM((nb, heads, tq, LANES), F32), pltpu.VMEM((nb, heads, tq, LANES), F32)],
        compiler_params=_params("parallel", "parallel", "arbitrary"),
        name="stick_breaking_attention",
    )(q, k, v)


def _odd_out_kernel(c_ref, d_ref, x_ref, w_ref, g_ref, beta_ref, rw_ref, rb_ref,
                    x1_ref, gate_ref, route_ref, cnt_ref, base_ref):
    tm = x_ref.shape[0]

    @pl.when(pl.program_id(0) == 0)
    def _():
        base_ref[...] = jnp.zeros_like(base_ref)

    x1 = _mix_ln(c_ref, d_ref, x_ref, w_ref, g_ref, beta_ref)
    x1_ref[...] = x1

    xh, xl = _split_bf16(x1)
    logits = _dot(xh, rw_ref[0]) + _dot(xl, rw_ref[0]) + _dot(xh, rw_ref[1]) + rb_ref[...]
    lane = lax.broadcasted_iota(jnp.int32, (tm, LANES), 1)
    logits = jnp.where(lane < N_EXPERTS, logits, -jnp.inf)
    m1 = jnp.max(logits, axis=1, keepdims=True)
    i1 = jnp.min(jnp.where(logits == m1, lane, LANES), axis=1, keepdims=True)
    sel1 = lane == i1
    rest = jnp.where(sel1, -jnp.inf, logits)
    m2 = jnp.max(rest, axis=1, keepdims=True)
    i2 = jnp.min(jnp.where(rest == m2, lane, LANES), axis=1, keepdims=True)
    sel2 = lane == i2
    e = jnp.exp(m2 - m1)
    g1 = 1.0 / (1.0 + e)
    g2 = e / (1.0 + e)
    first_lower = i1 < i2
    gate_ref[...] = jnp.where(lane == 0, jnp.where(first_lower, g1, g2),
                              jnp.where(lane == 1, jnp.where(first_lower, g2, g1), 0.0))

    sel_f = jnp.where(sel1 | sel2, 1.0, 0.0)
    row = lax.broadcasted_iota(jnp.int32, (tm, tm), 0)
    col = lax.broadcasted_iota(jnp.int32, (tm, tm), 1)
    before = jnp.where(col < row, 1.0, 0.0).astype(BF16)
    rank = base_ref[0:1, :] + _dot(before, sel_f.astype(BF16))
    e_lo = jnp.minimum(i1, i2)
    e_hi = jnp.maximum(i1, i2)
    rank_lo = jnp.sum(jnp.where(lane == e_lo, rank, 0.0), axis=1, keepdims=True)
    rank_hi = jnp.sum(jnp.where(lane == e_hi, rank, 0.0), axis=1, keepdims=True)
    route_ref[...] = jnp.where(lane == 0, e_lo, jnp.where(lane == 1, e_hi, jnp.where(
        lane == 2, rank_lo.astype(jnp.int32), jnp.where(lane == 3, rank_hi.astype(jnp.int32), 0))))
    base_ref[...] = base_ref[...] + jnp.sum(sel_f, axis=0, keepdims=True)
    cnt_ref[...] = base_ref[...].astype(jnp.int32)


def _odd_out(c, d, x, w, g, beta, rw, rb, tm):
    t = x.shape[0]
    lane_spec = _row_spec(tm, LANES)
    return pl.pallas_call(
        _odd_out_kernel,
        grid=(t // tm,),
        in_specs=[_row_spec(tm, HALF), _row_spec(tm, HALF), _row_spec(tm, D_MODEL),
                  _full_spec(w.shape), _full_spec(g.shape), _full_spec(beta.shape),
                  _full_spec(rw.shape), _full_spec(rb.shape)],
        out_specs=[_row_spec(tm, D_MODEL), lane_spec, lane_spec, _full_spec((8, LANES))],
        out_shape=[jax.ShapeDtypeStruct((t, D_MODEL), F32), jax.ShapeDtypeStruct((t, LANES), F32),
                   jax.ShapeDtypeStruct((t, LANES), jnp.int32), jax.ShapeDtypeStruct((8, LANES), jnp.int32)],
        scratch_shapes=[pltpu.VMEM((8, LANES), F32)],
        compiler_params=_params("arbitrary"),
        name="odd_out_proj_ln_router",
    )(c, d, x, w, g, beta, rw, rb)


def _for_rows(n, fn):
    def body(r, carry):
        fn(r)
        return carry
    lax.fori_loop(0, n, body, 0, unroll=8)


def _dispatch_kernel(pos_ref, x_ref, xs_hbm, slab_ref, sem):
    tm = x_ref.shape[0]
    for s in range(SLAB):
        slab_ref[pl.ds(s, tm, stride=SLAB), :] = x_ref[:, s * LANES:(s + 1) * LANES]

    def start(r):
        src = slab_ref.at[pl.ds(pl.multiple_of(r * SLAB, SLAB), SLAB)]
        for k in range(2):
            dst = pl.multiple_of(pos_ref[0, 0, k * tm + r] * SLAB, SLAB)
            pltpu.make_async_copy(src, xs_hbm.at[pl.ds(dst, SLAB)], sem).start(priority=k)

    _for_rows(tm, start)
    for _ in range(2):
        pltpu.make_async_copy(slab_ref, xs_hbm.at[pl.ds(0, tm * SLAB)], sem).wait()


def _dispatch(x1, pos, tm):
    t = x1.shape[0]
    return pl.pallas_call(
        _dispatch_kernel,
        grid=(t // tm,),
        in_specs=[pl.BlockSpec((1, 1, 2 * tm), lambda i: (i, 0, 0), memory_space=pltpu.SMEM),
                  _row_spec(tm, D_MODEL)],
        out_specs=pl.BlockSpec(memory_space=pl.ANY),
        out_shape=jax.ShapeDtypeStruct((2 * t * SLAB, LANES), F32),
        scratch_shapes=[pltpu.VMEM((tm * SLAB, LANES), F32), pltpu.SemaphoreType.DMA(())],
        compiler_params=_params("arbitrary"),
        name="moe_dispatch",
    )(pos, x1)


def _moe_kernel(tile_ref, expert_ref, lo_ref, hi_ref, xs_ref, w1_ref, w3_ref, w2_ref, y_ref, xb_ref, acc_ref):
    tm = xb_ref.shape[0]
    step = pl.program_id(0)
    lo = lo_ref[step]
    hi = hi_ref[step]

    @pl.when(hi > lo)
    def _():
        for s in range(SLAB):
            xb_ref[:, s * LANES:(s + 1) * LANES] = xs_ref[pl.ds(s, tm, stride=SLAB), :].astype(BF16)
        _swiglu_into(xb_ref, w1_ref, w3_ref, w2_ref, acc_ref)

        @pl.when(lo == 0)
        def _():
            for s in range(SLAB):
                y_ref[pl.ds(s, tm, stride=SLAB), :] = acc_ref[:, s * LANES:(s + 1) * LANES]

        @pl.when(lo > 0)
        def _():
            row = lax.broadcasted_iota(jnp.int32, (tm, LANES), 0)
            for s in range(SLAB):
                rows = pl.ds(s, tm, stride=SLAB)
                y_ref[rows, :] = jnp.where(row >= lo, acc_ref[:, s * LANES:(s + 1) * LANES], y_ref[rows, :])


def _moe(xs, step_tile, step_expert, step_lo, step_hi, w1, w3, w2, layer, tm):
    n_steps = step_tile.shape[0]
    w_spec = lambda shape: pl.BlockSpec((None, None) + shape[2:], lambda s, tile, ex, lo, hi: (layer, ex[s], 0, 0))
    row_spec = pl.BlockSpec((tm * SLAB, LANES), lambda s, tile, ex, lo, hi: (tile[s], 0))
    grid_spec = pltpu.PrefetchScalarGridSpec(
        num_scalar_prefetch=4,
        grid=(n_steps,),
        in_specs=[row_spec, w_spec(w1.shape), w_spec(w3.shape), w_spec(w2.shape)],
        out_specs=row_spec,
        scratch_shapes=[pltpu.VMEM((tm, D_MODEL), BF16), pltpu.VMEM((tm, D_MODEL), F32)],
    )
    return pl.pallas_call(
        _moe_kernel,
        grid_spec=grid_spec,
        out_shape=jax.ShapeDtypeStruct(xs.shape, F32),
        compiler_params=_params("arbitrary"),
        name="moe_grouped_swiglu",
    )(step_tile, step_expert, step_lo, step_hi, xs, w1, w3, w2)


def _combine_kernel(pos_ref, pos_next_ref, x_ref, gate_ref, g_ref, beta_ref, y_hbm, out_ref, ybuf, ffn_ref, sem):
    tm = x_ref.shape[0]
    i = pl.program_id(0)
    slot = i % 2

    def start_tile(p_ref, buf):
        def start(r):
            for k in range(2):
                src = pl.multiple_of(p_ref[0, 0, k * tm + r] * SLAB, SLAB)
                pltpu.make_async_copy(y_hbm.at[pl.ds(src, SLAB)],
                                      ybuf.at[buf, k, pl.ds(pl.multiple_of(r * SLAB, SLAB), SLAB)],
                                      sem.at[buf]).start(priority=k)
        _for_rows(tm, start)

    @pl.when(i == 0)
    def _():
        start_tile(pos_ref, 0)

    @pl.when(i + 1 < pl.num_programs(0))
    def _():
        start_tile(pos_next_ref, 1 - slot)

    for k in range(2):
        pltpu.make_async_copy(y_hbm.at[pl.ds(0, tm * SLAB)], ybuf.at[slot, k], sem.at[slot]).wait()
    gate = gate_ref[...]
    for s in range(SLAB):
        rows = pl.ds(s, tm, stride=SLAB)
        ffn_ref[:, s * LANES:(s + 1) * LANES] = (gate[:, 0:1] * ybuf[slot, 0, rows, :]
                                                 + gate[:, 1:2] * ybuf[slot, 1, rows, :])
    out_ref[...] = _ln(ALPHA * x_ref[...] + ffn_ref[...], g_ref[...], beta_ref[...])


def _combine(x1, y, pos, gate, g, beta, tm):
    t = x1.shape[0]
    nt = t // tm
    pos_spec = lambda f: pl.BlockSpec((1, 1, 2 * tm), f, memory_space=pltpu.SMEM)
    return pl.pallas_call(
        _combine_kernel,
        grid=(nt,),
        in_specs=[pos_spec(lambda i: (i, 0, 0)), pos_spec(lambda i: (jnp.minimum(i + 1, nt - 1), 0, 0)),
                  _row_spec(tm, D_MODEL), _row_spec(tm, LANES), _full_spec(g.shape), _full_spec(beta.shape),
                  pl.BlockSpec(memory_space=pl.ANY)],
        out_specs=_row_spec(tm, D_MODEL),
        out_shape=jax.ShapeDtypeStruct((t, D_MODEL), F32),
        scratch_shapes=[pltpu.VMEM((2, 2, tm * SLAB, LANES), F32), pltpu.VMEM((tm, D_MODEL), F32),
                        pltpu.SemaphoreType.DMA((2,))],
        compiler_params=_params("arbitrary"),
        name="moe_combine_ln",
    )(pos, pos, x1, gate, g, beta, y)


def _routing_tables(route, cnt, t, tm):
    counts = cnt[0, :N_EXPERTS]
    ends = jnp.cumsum(counts)
    starts = ends - counts
    experts = jnp.arange(N_EXPERTS, dtype=jnp.int32)

    def sorted_row(e, rank):
        return jnp.sum(jnp.where(e[:, None] == experts[None, :], starts[None, :], 0), axis=1) + rank

    pos = jnp.stack([sorted_row(route[:, 0], route[:, 2]), sorted_row(route[:, 1], route[:, 3])])
    nt = t // tm
    pos = pos.reshape(2, nt, tm).transpose(1, 0, 2).reshape(nt, 1, 2 * tm)

    n_steps = (2 * t) // tm + N_EXPERTS - 1
    first_tile = starts // tm
    n_visits = jnp.where(counts > 0, (ends - 1) // tm - first_tile + 1, 0)
    step_ends = jnp.cumsum(n_visits)
    total = step_ends[-1]
    s = jnp.minimum(jnp.arange(n_steps, dtype=jnp.int32), total - 1)
    step_expert = jnp.sum((s[:, None] >= step_ends[None, :]).astype(jnp.int32), axis=1)
    step_tile = first_tile[step_expert] + s - (step_ends - n_visits)[step_expert]
    lo = jnp.clip(starts[step_expert] - step_tile * tm, 0, tm)
    hi = jnp.clip(ends[step_expert] - step_tile * tm, 0, tm)
    hi = jnp.where(jnp.arange(n_steps) < total, hi, lo)
    as_i32 = lambda a: a.astype(jnp.int32)
    return pos.astype(jnp.int32), as_i32(step_tile), as_i32(step_expert), as_i32(lo), as_i32(hi)


def _row(v):
    return v.reshape(1, -1).astype(F32)


def _even_layer(x, bsz, s, w_in, gate_bias, conv_w, conv_b, norm_g, norm_b, head_g, w_out,
                ln1_g, ln1_b, w1, w3, w2, layer, ln2_g, ln2_b):
    main = 6 * HALF
    w_main = w_in[:, :main].astype(BF16)
    w_gate = jnp.pad(w_in[:, main:], ((0, 0), (0, LANES - 2 * MLSTM_HEADS))).astype(BF16)
    gb = jnp.pad(gate_bias, (0, LANES - 2 * MLSTM_HEADS)).reshape(1, LANES)
    glu, q, k, v, o, gates, kt, gates_t = _even_in(x, w_main, w_gate, gb, min(PROJ_TILE, x.shape[0]))
    shp = lambda a: a.reshape(bsz, s, a.shape[-1])
    a = _conv_branch(shp(glu), conv_w, conv_b, _row(norm_g), _row(norm_b), min(s, 512))
    hb = _mlstm_branch(shp(q), shp(k), kt, shp(v), shp(o), shp(gates), gates_t, _row(head_g))
    return _mix_ffn(a.reshape(-1, HALF), hb.reshape(-1, HALF), x, w_out.astype(BF16), _row(ln1_g), _row(ln1_b),
                    w1, w3, w2, layer, _row(ln2_g), _row(ln2_b), TOKEN_TILE)


def _odd_layer(x, bsz, s, w_in, v_g, v_b, w_s, b_s, w_out, ln1_g, ln1_b, router_w, router_b,
               w1, w3, w2, layer, ln2_g, ln2_b):
    t = x.shape[0]
    c, q, k, v = _odd_in(x, w_in.astype(BF16), _row(v_g), _row(v_b), w_s, b_s.T, min(PROJ_TILE, t))
    shp = lambda a: a.reshape(bsz, s, HALF)
    d = _sb_attention(shp(q), shp(k), shp(v), min(s, SB_TILE)).reshape(t, HALF)
    rw = jnp.pad(router_w, ((0, 0), (0, LANES - N_EXPERTS)))
    rw_hi = rw.astype(BF16)
    rw_lo = (rw - rw_hi.astype(F32)).astype(BF16)
    rb = jnp.pad(router_b, (0, LANES - N_EXPERTS)).reshape(1, LANES)
    x1, gate, route, cnt = _odd_out(c, d, x, w_out.astype(BF16), _row(ln1_g), _row(ln1_b),
                                    jnp.stack([rw_hi, rw_lo]), rb, min(PROJ_TILE, t))
    pos, step_tile, step_expert, step_lo, step_hi = _routing_tables(route, cnt, t, MOE_TILE)
    xs = _dispatch(x1, pos, MOE_TILE)
    y = _moe(xs, step_tile, step_expert, step_lo, step_hi, w1, w3, w2, layer, MOE_TILE)
    return _combine(x1, y, pos, gate, _row(ln2_g), _row(ln2_b), MOE_TILE)


def kernel(x, ab_w_in, ab_gate_bias, a_conv_w, a_conv_b, a_norm_g, a_norm_b, b_norm_g, ab_w_out, ab_ln1_g, ab_ln1_b, ffn_w1, ffn_w3, ffn_w2, ab_ln2_g, ab_ln2_b, cd_w_in, c_norm_g, c_norm_b, c_w_s, c_b_s, cd_w_out, cd_ln1_g, cd_ln1_b, router_w, router_b, moe_w1, moe_w3, moe_w2, cd_ln2_g, cd_ln2_b):
    bsz, s, _ = x.shape
    h = x.reshape(bsz * s, D_MODEL)
    ffn_w = [w.astype(BF16) for w in (ffn_w1, ffn_w3, ffn_w2)]
    moe_w = [w.astype(BF16) for w in (moe_w1, moe_w3, moe_w2)]
    for layer in range(DEPTH):
        j = layer // 2
        if layer % 2 == 0:
            h = _even_layer(h, bsz, s, ab_w_in[j], ab_gate_bias[j], a_conv_w[j], a_conv_b[j], a_norm_g[j],
                            a_norm_b[j], b_norm_g[j], ab_w_out[j], ab_ln1_g[j], ab_ln1_b[j],
                            *ffn_w, j, ab_ln2_g[j], ab_ln2_b[j])
        else:
            h = _odd_layer(h, bsz, s, cd_w_in[j], c_norm_g[j], c_norm_b[j], c_w_s[j], c_b_s[j], cd_w_out[j],
                           cd_ln1_g[j], cd_ln1_b[j], router_w[j], router_b[j],
                           *moe_w, j, cd_ln2_g[j], cd_ln2_b[j])
    return h.reshape(bsz, s, D_MODEL)
```

```python
import jax
import jax.numpy as jnp
from jax import lax
from jax.experimental import pallas as pl
from jax.experimental.pallas import tpu as pltpu

F32 = jnp.float32
BF16 = jnp.bfloat16

D_MODEL = 1024
DEPTH = 4
HALF = D_MODEL // 2
CONV_WIDTH = 31
CONV_ROWS = 16
CONV_HALO = 32
MLSTM_HEADS = 4
MLSTM_HEAD_DIM = HALF // MLSTM_HEADS
CHUNK = 128
GMLP_GROUPS = 4
SB_HEADS = 8
SB_HEAD_DIM = HALF // SB_HEADS
D_FF = 2816
FF_CHUNK = 512
N_EXPERTS = 8
ALPHA = (2 * DEPTH) ** 0.25
LN_EPS = 1e-5
LOG2E = 1.4426950408889634
MASKED_SCORE = -1e30
LANES = 128
SLAB = D_MODEL // LANES
V7X_VMEM_LIMIT = 56 * 1024 * 1024

PROJ_TILE = 1024
TOKEN_TILE = 512
MOE_TILE = 512
SB_TILE = 1024
SB_BATCH = 2
SB_KEY_TILE = 256


def _ln(x, g, b):
    mu = jnp.mean(x, axis=-1, keepdims=True)
    xc = x - mu
    var = jnp.mean(xc * xc, axis=-1, keepdims=True)
    return xc * lax.rsqrt(var + LN_EPS) * g + b


def _gelu(x):
    return 0.5 * x * (1.0 + lax.erf(x * (0.5 ** 0.5)))


def _dot(a, b):
    return jnp.dot(a, b, preferred_element_type=F32)


def _dot_nt(a, b):
    return lax.dot_general(a, b, (((1,), (1,)), ((), ())), preferred_element_type=F32)


def _split_bf16(x):
    hi = x.astype(BF16)
    lo = (x - hi.astype(F32)).astype(BF16)
    return hi, lo


def _params(*sem):
    return pltpu.CompilerParams(dimension_semantics=sem, vmem_limit_bytes=V7X_VMEM_LIMIT)


def _row_spec(tm, width):
    return pl.BlockSpec((tm, width), lambda i: (i, 0))


def _full_spec(shape):
    zeros = (0,) * len(shape)
    return pl.BlockSpec(shape, lambda *_: zeros)


def _even_in_kernel(x_ref, w_ref, wg_ref, gb_ref, glu_ref, q_ref, k_ref, v_ref, o_ref, g_ref, kt_ref, gt_ref):
    xb = x_ref[...].astype(BF16)

    def proj(c):
        return _dot(xb, w_ref[:, c * HALF:(c + 1) * HALF])

    glu_ref[...] = proj(0) * jax.nn.sigmoid(proj(1))
    q_ref[...] = proj(2).astype(BF16)
    k = proj(3) * MLSTM_HEAD_DIM ** -0.5
    k_ref[...] = k.astype(BF16)
    kt_ref[...] = k.T.astype(BF16)
    v_ref[...] = proj(4).astype(BF16)
    o_ref[...] = proj(5).astype(BF16)
    gates = _dot(xb, wg_ref[...]) + gb_ref[...]
    g_ref[...] = gates
    gt_ref[...] = gates.T[0:8, :]


def _even_in(x, w, wg, gb, tm):
    t = x.shape[0]
    half_bf = jax.ShapeDtypeStruct((t, HALF), BF16)
    return pl.pallas_call(
        _even_in_kernel,
        grid=(t // tm,),
        in_specs=[_row_spec(tm, D_MODEL), _full_spec(w.shape), _full_spec(wg.shape), _full_spec(gb.shape)],
        out_specs=[_row_spec(tm, HALF)] * 5 + [_row_spec(tm, LANES),
                   pl.BlockSpec((HALF, tm), lambda i: (0, i)), pl.BlockSpec((8, tm), lambda i: (0, i))],
        out_shape=[jax.ShapeDtypeStruct((t, HALF), F32), half_bf, half_bf, half_bf, half_bf,
                   jax.ShapeDtypeStruct((t, LANES), F32),
                   jax.ShapeDtypeStruct((HALF, t), BF16), jax.ShapeDtypeStruct((8, t), F32)],
        compiler_params=_params("parallel"),
        name="even_in_proj",
    )(x, w, wg, gb)


def _conv_kernel(prev_ref, cur_ref, w_ref, cb_ref, g_ref, b_ref, out_ref, win_ref, y_ref):
    nseq = cur_ref.shape[0]
    ts = cur_ref.shape[1]
    nblk = HALF // LANES
    first = pl.program_id(1) == 0
    for b in range(nseq):
        for c in range(nblk):
            lanes = slice(c * LANES, (c + 1) * LANES)
            sub = b * nblk + c
            win_ref[pl.ds(sub, CONV_HALO, stride=SLAB), :] = jnp.where(first, 0.0, prev_ref[b, :, lanes])
            win_ref[pl.ds(CONV_HALO * SLAB + sub, ts, stride=SLAB), :] = cur_ref[b, :, lanes]
    lead = CONV_HALO - (CONV_WIDTH - 1)
    rows = CONV_ROWS * SLAB

    def body(n, carry):
        base = pl.multiple_of(n * rows, rows)
        acc = jnp.tile(cb_ref[...], (CONV_ROWS, 1))
        for j in range(CONV_WIDTH):
            acc = acc + jnp.tile(w_ref[j], (CONV_ROWS, 1)) * win_ref[pl.ds(base + (lead + j) * SLAB, rows), :]
        y_ref[pl.ds(base, rows), :] = acc
        return carry

    lax.fori_loop(0, ts // CONV_ROWS, body, 0)
    for b in range(nseq):
        y = jnp.concatenate([y_ref[pl.ds(b * nblk + c, ts, stride=SLAB), :] for c in range(nblk)], axis=1)
        yn = _ln(y, g_ref[...], b_ref[...])
        out_ref[b] = (yn * jax.nn.sigmoid(yn)).astype(BF16)


def _conv_branch(glu, w, cb, g, b, ts):
    bsz, s, _ = glu.shape
    nseq = SLAB * LANES // HALF
    assert bsz % nseq == 0
    per = ts // CONV_HALO
    w_tiles = jnp.tile(jnp.pad(w, ((0, 32 - CONV_WIDTH), (0, 0))).reshape(32, HALF // LANES, LANES), (1, nseq, 1))
    cb_tile = jnp.tile(cb.reshape(HALF // LANES, LANES), (nseq, 1))
    return pl.pallas_call(
        _conv_kernel,
        grid=(bsz // nseq, s // ts),
        in_specs=[
            pl.BlockSpec((nseq, CONV_HALO, HALF), lambda bi, i: (bi, jnp.maximum(i * per - 1, 0), 0)),
            pl.BlockSpec((nseq, ts, HALF), lambda bi, i: (bi, i, 0)),
            _full_spec(w_tiles.shape), _full_spec(cb_tile.shape), _full_spec(g.shape), _full_spec(b.shape),
        ],
        out_specs=pl.BlockSpec((nseq, ts, HALF), lambda bi, i: (bi, i, 0)),
        out_shape=jax.ShapeDtypeStruct((bsz, s, HALF), BF16),
        scratch_shapes=[pltpu.VMEM(((ts + CONV_HALO) * SLAB, LANES), F32), pltpu.VMEM((ts * SLAB, LANES), F32)],
        compiler_params=_params("parallel", "parallel"),
        name="conv_branch",
    )(glu, glu, w_tiles, cb_tile, g, b)


def _mlstm_kernel(q_ref, k_ref, kt_ref, v_ref, o_ref, g_ref, gt_ref, hg_ref, out_ref, c_ref, n_ref, m_ref):
    L, H, d = CHUNK, MLSTM_HEADS, MLSTM_HEAD_DIM

    @pl.when(pl.program_id(1) == 0)
    def _():
        c_ref[...] = jnp.zeros_like(c_ref)
        n_ref[...] = jnp.zeros_like(n_ref)
        m_ref[...] = jnp.zeros_like(m_ref)

    row = lax.broadcasted_iota(jnp.int32, (L, L), 0)
    col = lax.broadcasted_iota(jnp.int32, (L, L), 1)
    causal = col <= row
    gates = g_ref[0]
    lf_hi, lf_lo = _split_bf16(jax.nn.log_sigmoid(gates))
    lower = jnp.where(causal, 1.0, 0.0).astype(BF16)
    bcum = _dot(lower, lf_hi) + _dot(lower, lf_lo)
    gates_r = gt_ref[...]
    lr_hi, lr_lo = _split_bf16(jax.nn.log_sigmoid(gates_r))
    upper = jnp.where(row <= col, 1.0, 0.0).astype(BF16)
    bcum_r = _dot(lr_hi, upper) + _dot(lr_lo, upper)

    for h in range(H):
        hs = slice(h * d, (h + 1) * d)
        qh, kh, vh = q_ref[0, :, hs], k_ref[0, :, hs], v_ref[0, :, hs]
        b_c = bcum[:, H + h:H + h + 1]
        i_r = gates_r[h:h + 1, :]
        b_r = bcum_r[H + h:H + h + 1, :]
        gtot = bcum[L - 1:L, H + h:H + h + 1]

        c_prev = c_ref[h]
        n_prev = n_ref[h, 0:1, :]
        m_prev = m_ref[h, 0:1, 0:1]

        log_w = gtot - b_r + i_r
        a = jnp.max(log_w, axis=1, keepdims=True)
        w = jnp.exp(log_w - a)
        kw_t = (kt_ref[hs, :].astype(F32) * w).astype(BF16)
        kv_loc = _dot(kw_t, vh)
        n_loc = _dot_nt(jnp.broadcast_to(w, (8, L)).astype(BF16), kt_ref[hs, :])[0:1, :]

        m_new = jnp.maximum(gtot + m_prev, a)
        s_old = jnp.exp(gtot + m_prev - m_new)
        s_new = jnp.exp(a - m_new)
        c_ref[h] = s_old * c_prev + s_new * kv_loc
        n_ref[h] = jnp.broadcast_to(s_old * n_prev + s_new * n_loc, (8, d))
        m_ref[h] = jnp.broadcast_to(m_new, (8, LANES))

        log_d = jnp.where(causal, b_c - b_r + i_r, -jnp.inf)
        log_inter = b_c + m_prev
        m_t = jnp.maximum(log_inter, jnp.max(log_d, axis=1, keepdims=True))
        d_intra = jnp.exp(log_d - m_t)
        s_inter = jnp.exp(log_inter - m_t)
        qk = _dot_nt(qh, kh) * d_intra
        num = _dot(qk.astype(BF16), vh) + s_inter * _dot(qh, c_prev.astype(BF16))
        den = jnp.sum(qk + s_inter * (qh.astype(F32) * n_prev), axis=1, keepdims=True)
        hh = num / jnp.maximum(jnp.abs(den), jnp.exp(-m_t))

        mu = jnp.mean(hh, axis=1, keepdims=True)
        hc = hh - mu
        var = jnp.mean(hc * hc, axis=1, keepdims=True)
        hn = hc * lax.rsqrt(var + LN_EPS) * hg_ref[:, hs]
        out_ref[0, :, hs] = (jax.nn.sigmoid(o_ref[0, :, hs].astype(F32)) * hn).astype(BF16)


def _mlstm_branch(q, k, kt, v, o, gates, gates_t, head_g):
    bsz, s, _ = q.shape
    nc = s // CHUNK
    blk = pl.BlockSpec((1, CHUNK, HALF), lambda bi, c: (bi, c, 0))
    return pl.pallas_call(
        _mlstm_kernel,
        grid=(bsz, nc),
        in_specs=[blk, blk, pl.BlockSpec((HALF, CHUNK), lambda bi, c: (0, bi * nc + c)), blk, blk,
                  pl.BlockSpec((1, CHUNK, LANES), lambda bi, c: (bi, c, 0)),
                  pl.BlockSpec((8, CHUNK), lambda bi, c: (0, bi * nc + c)),
                  _full_spec(head_g.shape)],
        out_specs=blk,
        out_shape=jax.ShapeDtypeStruct((bsz, s, HALF), BF16),
        scratch_shapes=[pltpu.VMEM((MLSTM_HEADS, MLSTM_HEAD_DIM, MLSTM_HEAD_DIM), F32),
                        pltpu.VMEM((MLSTM_HEADS, 8, MLSTM_HEAD_DIM), F32),
                        pltpu.VMEM((MLSTM_HEADS, 8, LANES), F32)],
        compiler_params=_params("parallel", "arbitrary"),
        name="mlstm_branch",
    )(q, k, kt, v, o, gates, gates_t, head_g)


def _mix_ln(a_ref, b_ref, x_ref, w_ref, g_ref, beta_ref):
    mix = _dot(a_ref[...], w_ref[0:HALF, :]) + _dot(b_ref[...], w_ref[HALF:, :])
    return _ln(ALPHA * x_ref[...] + mix, g_ref[...], beta_ref[...])


def _layer_spec(w, layer):
    zeros = (0,) * (w.ndim - 1)
    return pl.BlockSpec((None,) + w.shape[1:], lambda *_: (layer,) + zeros)


def _swiglu_into(xb_ref, w1_ref, w3_ref, w2_ref, acc_ref):
    for n, start in enumerate(range(0, D_FF, FF_CHUNK)):
        size = min(FF_CHUNK, D_FF - start)
        xb = xb_ref[...]
        h1 = _dot(xb, w1_ref[:, start:start + size])
        h3 = _dot(xb, w3_ref[:, start:start + size])
        hh = (h1 * jax.nn.sigmoid(h1) * h3).astype(BF16)
        out = _dot(hh, w2_ref[start:start + size, :])
        if n == 0:
            acc_ref[...] = out
        else:
            acc_ref[...] += out


def _mix_ffn_kernel(a_ref, b_ref, x_ref, wo_ref, g1_ref, beta1_ref, w1_ref, w3_ref, w2_ref, g2_ref, beta2_ref,
                    out_ref, x1_ref, xb_ref, acc_ref):
    x1_ref[...] = _mix_ln(a_ref, b_ref, x_ref, wo_ref, g1_ref, beta1_ref)
    xb_ref[...] = x1_ref[...].astype(BF16)
    _swiglu_into(xb_ref, w1_ref, w3_ref, w2_ref, acc_ref)
    out_ref[...] = _ln(ALPHA * x1_ref[...] + acc_ref[...], g2_ref[...], beta2_ref[...])


def _mix_ffn(a, b, x, wo, g1, beta1, w1, w3, w2, layer, g2, beta2, tm):
    t = x.shape[0]
    return pl.pallas_call(
        _mix_ffn_kernel,
        grid=(t // tm,),
        in_specs=[_row_spec(tm, HALF), _row_spec(tm, HALF), _row_spec(tm, D_MODEL),
                  _full_spec(wo.shape), _full_spec(g1.shape), _full_spec(beta1.shape),
                  _layer_spec(w1, layer), _layer_spec(w3, layer), _layer_spec(w2, layer),
                  _full_spec(g2.shape), _full_spec(beta2.shape)],
        out_specs=_row_spec(tm, D_MODEL),
        out_shape=jax.ShapeDtypeStruct((t, D_MODEL), F32),
        scratch_shapes=[pltpu.VMEM((tm, D_MODEL), F32), pltpu.VMEM((tm, D_MODEL), BF16),
                        pltpu.VMEM((tm, D_MODEL), F32)],
        compiler_params=_params("parallel"),
        name="mix_ln_dense_swiglu_ln",
    )(a, b, x, wo, g1, beta1, w1, w3, w2, g2, beta2)


def _odd_in_kernel(x_ref, w_ref, vg_ref, vb_ref, ws_ref, bs_ref, c_ref, q_ref, k_ref, v_ref):
    tm = x_ref.shape[0]
    xb = x_ref[...].astype(BF16)

    def proj(c):
        return _dot(xb, w_ref[:, c * HALF:(c + 1) * HALF])

    u = _gelu(proj(0))
    zn = _ln(_gelu(proj(1)), vg_ref[...], vb_ref[...]).astype(BF16)
    q_ref[...] = (proj(2) * SB_HEAD_DIM ** -0.5).astype(BF16)
    k_ref[...] = proj(3).astype(BF16)
    v_ref[...] = proj(4).astype(BF16)

    row = lax.broadcasted_iota(jnp.int32, (CHUNK, CHUNK), 0)
    col = lax.broadcasted_iota(jnp.int32, (CHUNK, CHUNK), 1)
    for g in range(GMLP_GROUPS):
        lanes = slice(g * LANES, (g + 1) * LANES)
        wc = jnp.where(col <= row, ws_ref[g], 0.0).astype(BF16)
        bias = bs_ref[:, g:g + 1]
        for r in range(tm // CHUNK):
            rows = slice(r * CHUNK, (r + 1) * CHUNK)
            sg = _dot(wc, zn[rows, lanes]) + bias
            c_ref[rows, lanes] = (u[rows, lanes] * sg).astype(BF16)


def _odd_in(x, w, vg, vb, ws, bs_t, tm):
    t = x.shape[0]
    half_bf = jax.ShapeDtypeStruct((t, HALF), BF16)
    return pl.pallas_call(
        _odd_in_kernel,
        grid=(t // tm,),
        in_specs=[_row_spec(tm, D_MODEL), _full_spec(w.shape), _full_spec(vg.shape), _full_spec(vb.shape),
                  _full_spec(ws.shape), _full_spec(bs_t.shape)],
        out_specs=[_row_spec(tm, HALF)] * 4,
        out_shape=[half_bf] * 4,
        compiler_params=_params("parallel"),
        name="odd_in_proj_gmlp",
    )(x, w, vg, vb, ws, bs_t)


def _sb_kernel(q_ref, k_ref, v_ref, out_ref, lsz_ref, sp_ref, run_ref, acc_ref):
    nb = q_ref.shape[0]
    tq = q_ref.shape[1]
    tk = SB_KEY_TILE
    ratio = tq // tk
    i = pl.program_id(2)
    lane = lax.broadcasted_iota(jnp.int32, (tq, LANES), 1)
    row = lax.broadcasted_iota(jnp.int32, (tk, tk), 0)
    col = lax.broadcasted_iota(jnp.int32, (tk, tk), 1)
    later = jnp.where(row > col, 1.0, 0.0).astype(BF16)
    qrow = lax.broadcasted_iota(jnp.int32, (tq, tk), 0)
    kcol = lax.broadcasted_iota(jnp.int32, (tq, tk), 1)
    last = ratio * (i + 1) - 1
    streams = []
    for b in range(nb):
        q = q_ref[b]
        zero = jnp.zeros_like(q)
        streams.append((b, 0, jnp.where(lane < SB_HEAD_DIM, q, zero)))
        streams.append((b, 1, jnp.where(lane >= SB_HEAD_DIM, q, zero)))

    def rows_of(k, diagonal):
        return slice((ratio - 1 - k) * tk, tq) if diagonal else slice(None)

    def scores(k, slot, diagonal):
        j = last - k
        rows = rows_of(k, diagonal)
        for b, h, qm in streams:
            kb = k_ref[b, pl.ds(pl.multiple_of(j * tk, tk), tk), :]
            z = _dot_nt(qm[rows], kb)
            sp = jnp.maximum(z, 0.0) + jnp.log(1.0 + jnp.exp2(jnp.abs(z) * -LOG2E))
            lsz = z - sp
            if diagonal:
                visible = (kcol + (ratio - 1 - k) * tk < qrow)[rows]
                sp = jnp.where(visible, sp, 0.0)
                lsz = jnp.where(visible, lsz, MASKED_SCORE)
            sp_ref[slot, b, h, rows] = sp.astype(BF16)
            run = run_ref[b, h, rows]
            for lt in range(tk // LANES):
                lanes = slice(lt * LANES, (lt + 1) * LANES)
                lsz_ref[slot, b, h, rows, lanes] = lsz[:, lanes] - run
            run_ref[b, h, rows] = run + jnp.sum(sp, axis=1, keepdims=True)

    def output(k, slot, diagonal=False):
        j = last - k
        rows = rows_of(k, diagonal)
        for b, h, _ in streams:
            vb = v_ref[b, pl.ds(pl.multiple_of(j * tk, tk), tk), :]
            att = jnp.exp2((lsz_ref[slot, b, h, rows] - _dot(sp_ref[slot, b, h, rows], later)) * LOG2E)
            acc_ref[b, h, rows] += _dot(att.astype(BF16), vb)

    run_ref[...] = jnp.zeros_like(run_ref)
    acc_ref[...] = jnp.zeros_like(acc_ref)
    scores(0, 0, True)
    for k in range(1, ratio):
        output(k - 1, (k - 1) % 2, True)
        scores(k, k % 2, True)

    def two_blocks(p, carry):
        output(2 * p - 1, 1)
        scores(2 * p, 0, False)
        output(2 * p, 0)
        scores(2 * p + 1, 1, False)
        return carry

    lax.fori_loop(ratio // 2, ratio * (i + 1) // 2, two_blocks, 0)
    output(last, 1)
    for b in range(nb):
        out_ref[b] = jnp.where(lane < SB_HEAD_DIM, acc_ref[b, 0], acc_ref[b, 1]).astype(BF16)


def _sb_attention(q, k, v, tq):
    bsz, s, _ = q.shape
    nb = SB_BATCH if bsz % SB_BATCH == 0 else 1
    kv_spec = pl.BlockSpec((nb, s, LANES), lambda bi, p, i: (bi, 0, p))
    q_spec = pl.BlockSpec((nb, tq, LANES), lambda bi, p, i: (bi, i, p))
    heads = LANES // SB_HEAD_DIM
    assert tq % (2 * SB_KEY_TILE) == 0, "the walk pairs key blocks, two scratch slots"
    return pl.pallas_call(
        _sb_kernel,
        grid=(bsz // nb, HALF // LANES, s // tq),
        in_specs=[q_spec, kv_spec, kv_spec],
        out_specs=q_spec,
        out_shape=jax.ShapeDtypeStruct((bsz, s, HALF), BF16),
        scratch_shapes=[pltpu.VMEM((2, nb, heads, tq, SB_KEY_TILE), F32),
                        pltpu.VMEM((2, nb, heads, tq, SB_KEY_TILE), BF16),
                        pltpu.VMEM((nb, heads, tq, LANES), F32), pltpu.VMEM((nb, heads, tq, LANES), F32)],
        compiler_params=_params("parallel", "parallel", "arbitrary"),
        name="stick_breaking_attention",
    )(q, k, v)


def _odd_out_kernel(c_ref, d_ref, x_ref, w_ref, g_ref, beta_ref, rw_ref, rb_ref,
                    x1_ref, gate_ref, route_ref, cnt_ref, base_ref):
    tm = x_ref.shape[0]

    @pl.when(pl.program_id(0) == 0)
    def _():
        base_ref[...] = jnp.zeros_like(base_ref)

    x1 = _mix_ln(c_ref, d_ref, x_ref, w_ref, g_ref, beta_ref)
    x1_ref[...] = x1

    xh, xl = _split_bf16(x1)
    logits = _dot(xh, rw_ref[0]) + _dot(xl, rw_ref[0]) + _dot(xh, rw_ref[1]) + rb_ref[...]
    lane = lax.broadcasted_iota(jnp.int32, (tm, LANES), 1)
    logits = jnp.where(lane < N_EXPERTS, logits, -jnp.inf)
    m1 = jnp.max(logits, axis=1, keepdims=True)
    i1 = jnp.min(jnp.where(logits == m1, lane, LANES), axis=1, keepdims=True)
    sel1 = lane == i1
    rest = jnp.where(sel1, -jnp.inf, logits)
    m2 = jnp.max(rest, axis=1, keepdims=True)
    i2 = jnp.min(jnp.where(rest == m2, lane, LANES), axis=1, keepdims=True)
    sel2 = lane == i2
    e = jnp.exp(m2 - m1)
    g1 = 1.0 / (1.0 + e)
    g2 = e / (1.0 + e)
    first_lower = i1 < i2
    gate_ref[...] = jnp.where(lane == 0, jnp.where(first_lower, g1, g2),
                              jnp.where(lane == 1, jnp.where(first_lower, g2, g1), 0.0))

    sel_f = jnp.where(sel1 | sel2, 1.0, 0.0)
    row = lax.broadcasted_iota(jnp.int32, (tm, tm), 0)
    col = lax.broadcasted_iota(jnp.int32, (tm, tm), 1)
    before = jnp.where(col < row, 1.0, 0.0).astype(BF16)
    rank = base_ref[0:1, :] + _dot(before, sel_f.astype(BF16))
    e_lo = jnp.minimum(i1, i2)
    e_hi = jnp.maximum(i1, i2)
    rank_lo = jnp.sum(jnp.where(lane == e_lo, rank, 0.0), axis=1, keepdims=True)
    rank_hi = jnp.sum(jnp.where(lane == e_hi, rank, 0.0), axis=1, keepdims=True)
    route_ref[...] = jnp.where(lane == 0, e_lo, jnp.where(lane == 1, e_hi, jnp.where(
        lane == 2, rank_lo.astype(jnp.int32), jnp.where(lane == 3, rank_hi.astype(jnp.int32), 0))))
    base_ref[...] = base_ref[...] + jnp.sum(sel_f, axis=0, keepdims=True)
    cnt_ref[...] = base_ref[...].astype(jnp.int32)


def _odd_out(c, d, x, w, g, beta, rw, rb, tm):
    t = x.shape[0]
    lane_spec = _row_spec(tm, LANES)
    return pl.pallas_call(
        _odd_out_kernel,
        grid=(t // tm,),
        in_specs=[_row_spec(tm, HALF), _row_spec(tm, HALF), _row_spec(tm, D_MODEL),
                  _full_spec(w.shape), _full_spec(g.shape), _full_spec(beta.shape),
                  _full_spec(rw.shape), _full_spec(rb.shape)],
        out_specs=[_row_spec(tm, D_MODEL), lane_spec, lane_spec, _full_spec((8, LANES))],
        out_shape=[jax.ShapeDtypeStruct((t, D_MODEL), F32), jax.ShapeDtypeStruct((t, LANES), F32),
                   jax.ShapeDtypeStruct((t, LANES), jnp.int32), jax.ShapeDtypeStruct((8, LANES), jnp.int32)],
        scratch_shapes=[pltpu.VMEM((8, LANES), F32)],
        compiler_params=_params("arbitrary"),
        name="odd_out_proj_ln_router",
    )(c, d, x, w, g, beta, rw, rb)


def _for_rows(n, fn):
    def body(r, carry):
        fn(r)
        return carry
    lax.fori_loop(0, n, body, 0, unroll=8)


def _dispatch_kernel(pos_ref, x_ref, xs_hbm, slab_ref, sem):
    tm = x_ref.shape[0]
    i = pl.program_id(0)
    buf = i % 2
    for s in range(SLAB):
        slab_ref[buf, pl.ds(s, tm, stride=SLAB), :] = x_ref[:, s * LANES:(s + 1) * LANES]

    def start(r):
        src = slab_ref.at[buf, pl.ds(pl.multiple_of(r * SLAB, SLAB), SLAB)]
        for k in range(2):
            dst = pl.multiple_of(pos_ref[0, 0, k * tm + r] * SLAB, SLAB)
            pltpu.make_async_copy(src, xs_hbm.at[pl.ds(dst, SLAB)], sem.at[buf]).start(priority=k)

    _for_rows(tm, start)

    def wait_step(b):
        for _ in range(2):
            pltpu.make_async_copy(slab_ref.at[b], xs_hbm.at[pl.ds(0, tm * SLAB)], sem.at[b]).wait()

    @pl.when(i > 0)
    def _():
        wait_step(1 - buf)

    @pl.when(i == pl.num_programs(0) - 1)
    def _():
        wait_step(buf)


def _dispatch(x1, pos, tm):
    t = x1.shape[0]
    return pl.pallas_call(
        _dispatch_kernel,
        grid=(t // tm,),
        in_specs=[pl.BlockSpec((1, 1, 2 * tm), lambda i: (i, 0, 0), memory_space=pltpu.SMEM),
                  _row_spec(tm, D_MODEL)],
        out_specs=pl.BlockSpec(memory_space=pl.ANY),
        out_shape=jax.ShapeDtypeStruct((2 * t * SLAB, LANES), F32),
        scratch_shapes=[pltpu.VMEM((2, tm * SLAB, LANES), F32), pltpu.SemaphoreType.DMA((2,))],
        compiler_params=_params("arbitrary"),
        name="moe_dispatch",
    )(pos, x1)


def _moe_kernel(tile_ref, expert_ref, lo_ref, hi_ref, xs_ref, w1_ref, w3_ref, w2_ref, y_ref, xb_ref, acc_ref):
    tm = xb_ref.shape[0]
    step = pl.program_id(0)
    lo = lo_ref[step]
    hi = hi_ref[step]

    @pl.when(hi > lo)
    def _():
        for s in range(SLAB):
            xb_ref[:, s * LANES:(s + 1) * LANES] = xs_ref[pl.ds(s, tm, stride=SLAB), :].astype(BF16)
        _swiglu_into(xb_ref, w1_ref, w3_ref, w2_ref, acc_ref)

        @pl.when(lo == 0)
        def _():
            for s in range(SLAB):
                y_ref[pl.ds(s, tm, stride=SLAB), :] = acc_ref[:, s * LANES:(s + 1) * LANES]

        @pl.when(lo > 0)
        def _():
            row = lax.broadcasted_iota(jnp.int32, (tm, LANES), 0)
            for s in range(SLAB):
                rows = pl.ds(s, tm, stride=SLAB)
                y_ref[rows, :] = jnp.where(row >= lo, acc_ref[:, s * LANES:(s + 1) * LANES], y_ref[rows, :])


def _moe(xs, step_tile, step_expert, step_lo, step_hi, w1, w3, w2, layer, tm):
    n_steps = step_tile.shape[0]
    w_spec = lambda shape: pl.BlockSpec((None, None) + shape[2:], lambda s, tile, ex, lo, hi: (layer, ex[s], 0, 0))
    row_spec = pl.BlockSpec((tm * SLAB, LANES), lambda s, tile, ex, lo, hi: (tile[s], 0))
    grid_spec = pltpu.PrefetchScalarGridSpec(
        num_scalar_prefetch=4,
        grid=(n_steps,),
        in_specs=[row_spec, w_spec(w1.shape), w_spec(w3.shape), w_spec(w2.shape)],
        out_specs=row_spec,
        scratch_shapes=[pltpu.VMEM((tm, D_MODEL), BF16), pltpu.VMEM((tm, D_MODEL), F32)],
    )
    return pl.pallas_call(
        _moe_kernel,
        grid_spec=grid_spec,
        out_shape=jax.ShapeDtypeStruct(xs.shape, F32),
        compiler_params=_params("arbitrary"),
        name="moe_grouped_swiglu",
    )(step_tile, step_expert, step_lo, step_hi, xs, w1, w3, w2)


def _combine_kernel(pos_ref, pos_next_ref, x_ref, gate_ref, g_ref, beta_ref, y_hbm, out_ref, ybuf, ffn_ref, sem):
    tm = x_ref.shape[0]
    i = pl.program_id(0)
    slot = i % 2

    def start_tile(p_ref, buf):
        def start(r):
            for k in range(2):
                src = pl.multiple_of(p_ref[0, 0, k * tm + r] * SLAB, SLAB)
                pltpu.make_async_copy(y_hbm.at[pl.ds(src, SLAB)],
                                      ybuf.at[buf, k, pl.ds(pl.multiple_of(r * SLAB, SLAB), SLAB)],
                                      sem.at[buf]).start(priority=k)
        _for_rows(tm, start)

    @pl.when(i == 0)
    def _():
        start_tile(pos_ref, 0)

    @pl.when(i + 1 < pl.num_programs(0))
    def _():
        start_tile(pos_next_ref, 1 - slot)

    for k in range(2):
        pltpu.make_async_copy(y_hbm.at[pl.ds(0, tm * SLAB)], ybuf.at[slot, k], sem.at[slot]).wait()
    gate = gate_ref[...]
    for s in range(SLAB):
        rows = pl.ds(s, tm, stride=SLAB)
        ffn_ref[:, s * LANES:(s + 1) * LANES] = (gate[:, 0:1] * ybuf[slot, 0, rows, :]
                                                 + gate[:, 1:2] * ybuf[slot, 1, rows, :])
    out_ref[...] = _ln(ALPHA * x_ref[...] + ffn_ref[...], g_ref[...], beta_ref[...])


def _combine(x1, y, pos, gate, g, beta, tm):
    t = x1.shape[0]
    nt = t // tm
    pos_spec = lambda f: pl.BlockSpec((1, 1, 2 * tm), f, memory_space=pltpu.SMEM)
    return pl.pallas_call(
        _combine_kernel,
        grid=(nt,),
        in_specs=[pos_spec(lambda i: (i, 0, 0)), pos_spec(lambda i: (jnp.minimum(i + 1, nt - 1), 0, 0)),
                  _row_spec(tm, D_MODEL), _row_spec(tm, LANES), _full_spec(g.shape), _full_spec(beta.shape),
                  pl.BlockSpec(memory_space=pl.ANY)],
        out_specs=_row_spec(tm, D_MODEL),
        out_shape=jax.ShapeDtypeStruct((t, D_MODEL), F32),
        scratch_shapes=[pltpu.VMEM((2, 2, tm * SLAB, LANES), F32), pltpu.VMEM((tm, D_MODEL), F32),
                        pltpu.SemaphoreType.DMA((2,))],
        compiler_params=_params("arbitrary"),
        name="moe_combine_ln",
    )(pos, pos, x1, gate, g, beta, y)


def _routing_tables(route, cnt, t, tm):
    counts = cnt[0, :N_EXPERTS]
    ends = jnp.cumsum(counts)
    starts = ends - counts
    experts = jnp.arange(N_EXPERTS, dtype=jnp.int32)

    def sorted_row(e, rank):
        return jnp.sum(jnp.where(e[:, None] == experts[None, :], starts[None, :], 0), axis=1) + rank

    pos = jnp.stack([sorted_row(route[:, 0], route[:, 2]), sorted_row(route[:, 1], route[:, 3])])
    nt = t // tm
    pos = pos.reshape(2, nt, tm).transpose(1, 0, 2).reshape(nt, 1, 2 * tm)

    n_steps = (2 * t) // tm + N_EXPERTS - 1
    first_tile = starts // tm
    n_visits = jnp.where(counts > 0, (ends - 1) // tm - first_tile + 1, 0)
    step_ends = jnp.cumsum(n_visits)
    total = step_ends[-1]
    s = jnp.minimum(jnp.arange(n_steps, dtype=jnp.int32), total - 1)
    step_expert = jnp.sum((s[:, None] >= step_ends[None, :]).astype(jnp.int32), axis=1)
    step_tile = first_tile[step_expert] + s - (step_ends - n_visits)[step_expert]
    lo = jnp.clip(starts[step_expert] - step_tile * tm, 0, tm)
    hi = jnp.clip(ends[step_expert] - step_tile * tm, 0, tm)
    hi = jnp.where(jnp.arange(n_steps) < total, hi, lo)
    as_i32 = lambda a: a.astype(jnp.int32)
    return pos.astype(jnp.int32), as_i32(step_tile), as_i32(step_expert), as_i32(lo), as_i32(hi)


def _row(v):
    return v.reshape(1, -1).astype(F32)


def _even_layer(x, bsz, s, w_in, gate_bias, conv_w, conv_b, norm_g, norm_b, head_g, w_out,
                ln1_g, ln1_b, w1, w3, w2, layer, ln2_g, ln2_b):
    main = 6 * HALF
    w_main = w_in[:, :main].astype(BF16)
    w_gate = jnp.pad(w_in[:, main:], ((0, 0), (0, LANES - 2 * MLSTM_HEADS))).astype(BF16)
    gb = jnp.pad(gate_bias, (0, LANES - 2 * MLSTM_HEADS)).reshape(1, LANES)
    glu, q, k, v, o, gates, kt, gates_t = _even_in(x, w_main, w_gate, gb, min(PROJ_TILE, x.shape[0]))
    shp = lambda a: a.reshape(bsz, s, a.shape[-1])
    a = _conv_branch(shp(glu), conv_w, conv_b, _row(norm_g), _row(norm_b), min(s, 512))
    hb = _mlstm_branch(shp(q), shp(k), kt, shp(v), shp(o), shp(gates), gates_t, _row(head_g))
    return _mix_ffn(a.reshape(-1, HALF), hb.reshape(-1, HALF), x, w_out.astype(BF16), _row(ln1_g), _row(ln1_b),
                    w1, w3, w2, layer, _row(ln2_g), _row(ln2_b), TOKEN_TILE)


def _odd_layer(x, bsz, s, w_in, v_g, v_b, w_s, b_s, w_out, ln1_g, ln1_b, router_w, router_b,
               w1, w3, w2, layer, ln2_g, ln2_b):
    t = x.shape[0]
    c, q, k, v = _odd_in(x, w_in.astype(BF16), _row(v_g), _row(v_b), w_s, b_s.T, min(PROJ_TILE, t))
    shp = lambda a: a.reshape(bsz, s, HALF)
    d = _sb_attention(shp(q), shp(k), shp(v), min(s, SB_TILE)).reshape(t, HALF)
    rw = jnp.pad(router_w, ((0, 0), (0, LANES - N_EXPERTS)))
    rw_hi = rw.astype(BF16)
    rw_lo = (rw - rw_hi.astype(F32)).astype(BF16)
    rb = jnp.pad(router_b, (0, LANES - N_EXPERTS)).reshape(1, LANES)
    x1, gate, route, cnt = _odd_out(c, d, x, w_out.astype(BF16), _row(ln1_g), _row(ln1_b),
                                    jnp.stack([rw_hi, rw_lo]), rb, min(PROJ_TILE, t))
    pos, step_tile, step_expert, step_lo, step_hi = _routing_tables(route, cnt, t, MOE_TILE)
    xs = _dispatch(x1, pos, MOE_TILE)
    y = _moe(xs, step_tile, step_expert, step_lo, step_hi, w1, w3, w2, layer, MOE_TILE)
    return _combine(x1, y, pos, gate, _row(ln2_g), _row(ln2_b), MOE_TILE)


def kernel(x, ab_w_in, ab_gate_bias, a_conv_w, a_conv_b, a_norm_g, a_norm_b, b_norm_g, ab_w_out, ab_ln1_g, ab_ln1_b, ffn_w1, ffn_w3, ffn_w2, ab_ln2_g, ab_ln2_b, cd_w_in, c_norm_g, c_norm_b, c_w_s, c_b_s, cd_w_out, cd_ln1_g, cd_ln1_b, router_w, router_b, moe_w1, moe_w3, moe_w2, cd_ln2_g, cd_ln2_b):
    bsz, s, _ = x.shape
    h = x.reshape(bsz * s, D_MODEL)
    ffn_w = [w.astype(BF16) for w in (ffn_w1, ffn_w3, ffn_w2)]
    moe_w = [w.astype(BF16) for w in (moe_w1, moe_w3, moe_w2)]
    for layer in range(DEPTH):
        j = layer // 2
        if layer % 2 == 0:
            h = _even_layer(h, bsz, s, ab_w_in[j], ab_gate_bias[j], a_conv_w[j], a_conv_b[j], a_norm_g[j],
                            a_norm_b[j], b_norm_g[j], ab_w_out[j], ab_ln1_g[j], ab_ln1_b[j],
                            *ffn_w, j, ab_ln2_g[j], ab_ln2_b[j])
        else:
            h = _odd_layer(h, bsz, s, cd_w_in[j], c_norm_g[j], c_norm_b[j], c_w_s[j], c_b_s[j], cd_w_out[j],
                           cd_ln1_g[j], cd_ln1_b[j], router_w[j], router_b[j],
                           *moe_w, j, cd_ln2_g[j], cd_ln2_b[j])
    return h.reshape(bsz, s, D_MODEL)
```

```python
import jax
import jax.numpy as jnp
from jax import lax
from jax.experimental import pallas as pl
from jax.experimental.pallas import tpu as pltpu

F32 = jnp.float32
BF16 = jnp.bfloat16

D_MODEL = 1024
DEPTH = 4
HALF = D_MODEL // 2
CONV_WIDTH = 31
CONV_ROWS = 16
CONV_HALO = 32
MLSTM_HEADS = 4
MLSTM_HEAD_DIM = HALF // MLSTM_HEADS
CHUNK = 128
GMLP_GROUPS = 4
SB_HEADS = 8
SB_HEAD_DIM = HALF // SB_HEADS
D_FF = 2816
FF_CHUNK = 512
N_EXPERTS = 8
ALPHA = (2 * DEPTH) ** 0.25
LN_EPS = 1e-5
LOG2E = 1.4426950408889634
MASKED_SCORE = -1e30
LANES = 128
SLAB = D_MODEL // LANES
V7X_VMEM_LIMIT = 56 * 1024 * 1024

PROJ_TILE = 1024
TOKEN_TILE = 512
MOE_TILE = 512
SB_TILE = 1024
SB_BATCH = 2
SB_KEY_TILE = 256


def _ln(x, g, b):
    mu = jnp.mean(x, axis=-1, keepdims=True)
    xc = x - mu
    var = jnp.mean(xc * xc, axis=-1, keepdims=True)
    return xc * lax.rsqrt(var + LN_EPS) * g + b


def _gelu(x):
    return 0.5 * x * (1.0 + lax.erf(x * (0.5 ** 0.5)))


def _dot(a, b):
    return jnp.dot(a, b, preferred_element_type=F32)


def _dot_nt(a, b):
    return lax.dot_general(a, b, (((1,), (1,)), ((), ())), preferred_element_type=F32)


def _split_bf16(x):
    hi = x.astype(BF16)
    lo = (x - hi.astype(F32)).astype(BF16)
    return hi, lo


def _params(*sem):
    return pltpu.CompilerParams(dimension_semantics=sem, vmem_limit_bytes=V7X_VMEM_LIMIT)


def _row_spec(tm, width):
    return pl.BlockSpec((tm, width), lambda i: (i, 0))


def _full_spec(shape):
    zeros = (0,) * len(shape)
    return pl.BlockSpec(shape, lambda *_: zeros)


def _even_in_kernel(x_ref, w_ref, wg_ref, gb_ref, glu_ref, q_ref, k_ref, v_ref, o_ref, g_ref, kt_ref, gt_ref):
    xb = x_ref[...].astype(BF16)

    def proj(c):
        return _dot(xb, w_ref[:, c * HALF:(c + 1) * HALF])

    glu_ref[...] = proj(0) * jax.nn.sigmoid(proj(1))
    q_ref[...] = proj(2).astype(BF16)
    k = proj(3) * MLSTM_HEAD_DIM ** -0.5
    k_ref[...] = k.astype(BF16)
    kt_ref[...] = k.T.astype(BF16)
    v_ref[...] = proj(4).astype(BF16)
    o_ref[...] = proj(5).astype(BF16)
    gates = _dot(xb, wg_ref[...]) + gb_ref[...]
    g_ref[...] = gates
    gt_ref[...] = gates.T[0:8, :]


def _even_in(x, w, wg, gb, tm):
    t = x.shape[0]
    half_bf = jax.ShapeDtypeStruct((t, HALF), BF16)
    return pl.pallas_call(
        _even_in_kernel,
        grid=(t // tm,),
        in_specs=[_row_spec(tm, D_MODEL), _full_spec(w.shape), _full_spec(wg.shape), _full_spec(gb.shape)],
        out_specs=[_row_spec(tm, HALF)] * 5 + [_row_spec(tm, LANES),
                   pl.BlockSpec((HALF, tm), lambda i: (0, i)), pl.BlockSpec((8, tm), lambda i: (0, i))],
        out_shape=[jax.ShapeDtypeStruct((t, HALF), F32), half_bf, half_bf, half_bf, half_bf,
                   jax.ShapeDtypeStruct((t, LANES), F32),
                   jax.ShapeDtypeStruct((HALF, t), BF16), jax.ShapeDtypeStruct((8, t), F32)],
        compiler_params=_params("parallel"),
        name="even_in_proj",
    )(x, w, wg, gb)


def _conv_kernel(prev_ref, cur_ref, w_ref, cb_ref, g_ref, b_ref, out_ref, win_ref, y_ref):
    nseq = cur_ref.shape[0]
    ts = cur_ref.shape[1]
    nblk = HALF // LANES
    first = pl.program_id(1) == 0
    for b in range(nseq):
        for c in range(nblk):
            lanes = slice(c * LANES, (c + 1) * LANES)
            sub = b * nblk + c
            win_ref[pl.ds(sub, CONV_HALO, stride=SLAB), :] = jnp.where(first, 0.0, prev_ref[b, :, lanes])
            win_ref[pl.ds(CONV_HALO * SLAB + sub, ts, stride=SLAB), :] = cur_ref[b, :, lanes]
    lead = CONV_HALO - (CONV_WIDTH - 1)
    rows = CONV_ROWS * SLAB

    def body(n, carry):
        base = pl.multiple_of(n * rows, rows)
        acc = jnp.tile(cb_ref[...], (CONV_ROWS, 1))
        for j in range(CONV_WIDTH):
            acc = acc + jnp.tile(w_ref[j], (CONV_ROWS, 1)) * win_ref[pl.ds(base + (lead + j) * SLAB, rows), :]
        y_ref[pl.ds(base, rows), :] = acc
        return carry

    lax.fori_loop(0, ts // CONV_ROWS, body, 0)
    for b in range(nseq):
        y = jnp.concatenate([y_ref[pl.ds(b * nblk + c, ts, stride=SLAB), :] for c in range(nblk)], axis=1)
        yn = _ln(y, g_ref[...], b_ref[...])
        out_ref[b] = (yn * jax.nn.sigmoid(yn)).astype(BF16)


def _conv_branch(glu, w, cb, g, b, ts):
    bsz, s, _ = glu.shape
    nseq = SLAB * LANES // HALF
    assert bsz % nseq == 0
    per = ts // CONV_HALO
    w_tiles = jnp.tile(jnp.pad(w, ((0, 32 - CONV_WIDTH), (0, 0))).reshape(32, HALF // LANES, LANES), (1, nseq, 1))
    cb_tile = jnp.tile(cb.reshape(HALF // LANES, LANES), (nseq, 1))
    return pl.pallas_call(
        _conv_kernel,
        grid=(bsz // nseq, s // ts),
        in_specs=[
            pl.BlockSpec((nseq, CONV_HALO, HALF), lambda bi, i: (bi, jnp.maximum(i * per - 1, 0), 0)),
            pl.BlockSpec((nseq, ts, HALF), lambda bi, i: (bi, i, 0)),
            _full_spec(w_tiles.shape), _full_spec(cb_tile.shape), _full_spec(g.shape), _full_spec(b.shape),
        ],
        out_specs=pl.BlockSpec((nseq, ts, HALF), lambda bi, i: (bi, i, 0)),
        out_shape=jax.ShapeDtypeStruct((bsz, s, HALF), BF16),
        scratch_shapes=[pltpu.VMEM(((ts + CONV_HALO) * SLAB, LANES), F32), pltpu.VMEM((ts * SLAB, LANES), F32)],
        compiler_params=_params("parallel", "parallel"),
        name="conv_branch",
    )(glu, glu, w_tiles, cb_tile, g, b)


def _mlstm_kernel(q_ref, k_ref, kt_ref, v_ref, o_ref, g_ref, gt_ref, hg_ref, out_ref, c_ref, n_ref, m_ref):
    L, H, d = CHUNK, MLSTM_HEADS, MLSTM_HEAD_DIM

    @pl.when(pl.program_id(1) == 0)
    def _():
        c_ref[...] = jnp.zeros_like(c_ref)
        n_ref[...] = jnp.zeros_like(n_ref)
        m_ref[...] = jnp.zeros_like(m_ref)

    row = lax.broadcasted_iota(jnp.int32, (L, L), 0)
    col = lax.broadcasted_iota(jnp.int32, (L, L), 1)
    causal = col <= row
    gates = g_ref[0]
    lf_hi, lf_lo = _split_bf16(jax.nn.log_sigmoid(gates))
    lower = jnp.where(causal, 1.0, 0.0).astype(BF16)
    bcum = _dot(lower, lf_hi) + _dot(lower, lf_lo)
    gates_r = gt_ref[...]
    lr_hi, lr_lo = _split_bf16(jax.nn.log_sigmoid(gates_r))
    upper = jnp.where(row <= col, 1.0, 0.0).astype(BF16)
    bcum_r = _dot(lr_hi, upper) + _dot(lr_lo, upper)

    for h in range(H):
        hs = slice(h * d, (h + 1) * d)
        qh, kh, vh = q_ref[0, :, hs], k_ref[0, :, hs], v_ref[0, :, hs]
        b_c = bcum[:, H + h:H + h + 1]
        i_r = gates_r[h:h + 1, :]
        b_r = bcum_r[H + h:H + h + 1, :]
        gtot = bcum[L - 1:L, H + h:H + h + 1]

        c_prev = c_ref[h]
        n_prev = n_ref[h, 0:1, :]
        m_prev = m_ref[h, 0:1, 0:1]

        log_w = gtot - b_r + i_r
        a = jnp.max(log_w, axis=1, keepdims=True)
        w = jnp.exp(log_w - a)
        kw_t = (kt_ref[hs, :].astype(F32) * w).astype(BF16)
        kv_loc = _dot(kw_t, vh)
        n_loc = _dot_nt(jnp.broadcast_to(w, (8, L)).astype(BF16), kt_ref[hs, :])[0:1, :]

        m_new = jnp.maximum(gtot + m_prev, a)
        s_old = jnp.exp(gtot + m_prev - m_new)
        s_new = jnp.exp(a - m_new)
        c_ref[h] = s_old * c_prev + s_new * kv_loc
        n_ref[h] = jnp.broadcast_to(s_old * n_prev + s_new * n_loc, (8, d))
        m_ref[h] = jnp.broadcast_to(m_new, (8, LANES))

        log_d = jnp.where(causal, b_c - b_r + i_r, -jnp.inf)
        log_inter = b_c + m_prev
        m_t = jnp.maximum(log_inter, jnp.max(log_d, axis=1, keepdims=True))
        d_intra = jnp.exp(log_d - m_t)
        s_inter = jnp.exp(log_inter - m_t)
        qk = _dot_nt(qh, kh) * d_intra
        num = _dot(qk.astype(BF16), vh) + s_inter * _dot(qh, c_prev.astype(BF16))
        den = jnp.sum(qk + s_inter * (qh.astype(F32) * n_prev), axis=1, keepdims=True)
        hh = num / jnp.maximum(jnp.abs(den), jnp.exp(-m_t))

        mu = jnp.mean(hh, axis=1, keepdims=True)
        var = jnp.mean(hh * hh, axis=1, keepdims=True) - mu * mu
        hn = (hh - mu) * lax.rsqrt(var + LN_EPS) * hg_ref[:, hs]
        out_ref[0, :, hs] = (jax.nn.sigmoid(o_ref[0, :, hs].astype(F32)) * hn).astype(BF16)


def _mlstm_branch(q, k, kt, v, o, gates, gates_t, head_g):
    bsz, s, _ = q.shape
    nc = s // CHUNK
    blk = pl.BlockSpec((1, CHUNK, HALF), lambda bi, c: (bi, c, 0))
    return pl.pallas_call(
        _mlstm_kernel,
        grid=(bsz, nc),
        in_specs=[blk, blk, pl.BlockSpec((HALF, CHUNK), lambda bi, c: (0, bi * nc + c)), blk, blk,
                  pl.BlockSpec((1, CHUNK, LANES), lambda bi, c: (bi, c, 0)),
                  pl.BlockSpec((8, CHUNK), lambda bi, c: (0, bi * nc + c)),
                  _full_spec(head_g.shape)],
        out_specs=blk,
        out_shape=jax.ShapeDtypeStruct((bsz, s, HALF), BF16),
        scratch_shapes=[pltpu.VMEM((MLSTM_HEADS, MLSTM_HEAD_DIM, MLSTM_HEAD_DIM), F32),
                        pltpu.VMEM((MLSTM_HEADS, 8, MLSTM_HEAD_DIM), F32),
                        pltpu.VMEM((MLSTM_HEADS, 8, LANES), F32)],
        compiler_params=_params("parallel", "arbitrary"),
        name="mlstm_branch",
    )(q, k, kt, v, o, gates, gates_t, head_g)


def _mix_ln(a_ref, b_ref, x_ref, w_ref, g_ref, beta_ref):
    mix = _dot(a_ref[...], w_ref[0:HALF, :]) + _dot(b_ref[...], w_ref[HALF:, :])
    return _ln(ALPHA * x_ref[...] + mix, g_ref[...], beta_ref[...])


def _layer_spec(w, layer):
    zeros = (0,) * (w.ndim - 1)
    return pl.BlockSpec((None,) + w.shape[1:], lambda *_: (layer,) + zeros)


def _swiglu_into(xb_ref, w1_ref, w3_ref, w2_ref, acc_ref):
    for n, start in enumerate(range(0, D_FF, FF_CHUNK)):
        size = min(FF_CHUNK, D_FF - start)
        xb = xb_ref[...]
        h1 = _dot(xb, w1_ref[:, start:start + size])
        h3 = _dot(xb, w3_ref[:, start:start + size])
        hh = (h1 * jax.nn.sigmoid(h1) * h3).astype(BF16)
        out = _dot(hh, w2_ref[start:start + size, :])
        if n == 0:
            acc_ref[...] = out
        else:
            acc_ref[...] += out


def _mix_ffn_kernel(a_ref, b_ref, x_ref, wo_ref, g1_ref, beta1_ref, w1_ref, w3_ref, w2_ref, g2_ref, beta2_ref,
                    out_ref, x1_ref, xb_ref, acc_ref):
    x1_ref[...] = _mix_ln(a_ref, b_ref, x_ref, wo_ref, g1_ref, beta1_ref)
    xb_ref[...] = x1_ref[...].astype(BF16)
    _swiglu_into(xb_ref, w1_ref, w3_ref, w2_ref, acc_ref)
    out_ref[...] = _ln(ALPHA * x1_ref[...] + acc_ref[...], g2_ref[...], beta2_ref[...])


def _mix_ffn(a, b, x, wo, g1, beta1, w1, w3, w2, layer, g2, beta2, tm):
    t = x.shape[0]
    return pl.pallas_call(
        _mix_ffn_kernel,
        grid=(t // tm,),
        in_specs=[_row_spec(tm, HALF), _row_spec(tm, HALF), _row_spec(tm, D_MODEL),
                  _full_spec(wo.shape), _full_spec(g1.shape), _full_spec(beta1.shape),
                  _layer_spec(w1, layer), _layer_spec(w3, layer), _layer_spec(w2, layer),
                  _full_spec(g2.shape), _full_spec(beta2.shape)],
        out_specs=_row_spec(tm, D_MODEL),
        out_shape=jax.ShapeDtypeStruct((t, D_MODEL), F32),
        scratch_shapes=[pltpu.VMEM((tm, D_MODEL), F32), pltpu.VMEM((tm, D_MODEL), BF16),
                        pltpu.VMEM((tm, D_MODEL), F32)],
        compiler_params=_params("parallel"),
        name="mix_ln_dense_swiglu_ln",
    )(a, b, x, wo, g1, beta1, w1, w3, w2, g2, beta2)


def _odd_in_kernel(x_ref, w_ref, vg_ref, vb_ref, ws_ref, bs_ref, c_ref, q_ref, k_ref, v_ref):
    tm = x_ref.shape[0]
    xb = x_ref[...].astype(BF16)

    def proj(c):
        return _dot(xb, w_ref[:, c * HALF:(c + 1) * HALF])

    u = _gelu(proj(0))
    zn = _ln(_gelu(proj(1)), vg_ref[...], vb_ref[...]).astype(BF16)
    q_ref[...] = (proj(2) * SB_HEAD_DIM ** -0.5).astype(BF16)
    k_ref[...] = proj(3).astype(BF16)
    v_ref[...] = proj(4).astype(BF16)

    row = lax.broadcasted_iota(jnp.int32, (CHUNK, CHUNK), 0)
    col = lax.broadcasted_iota(jnp.int32, (CHUNK, CHUNK), 1)
    for g in range(GMLP_GROUPS):
        lanes = slice(g * LANES, (g + 1) * LANES)
        wc = jnp.where(col <= row, ws_ref[g], 0.0).astype(BF16)
        bias = bs_ref[:, g:g + 1]
        for r in range(tm // CHUNK):
            rows = slice(r * CHUNK, (r + 1) * CHUNK)
            sg = _dot(wc, zn[rows, lanes]) + bias
            c_ref[rows, lanes] = (u[rows, lanes] * sg).astype(BF16)


def _odd_in(x, w, vg, vb, ws, bs_t, tm):
    t = x.shape[0]
    half_bf = jax.ShapeDtypeStruct((t, HALF), BF16)
    return pl.pallas_call(
        _odd_in_kernel,
        grid=(t // tm,),
        in_specs=[_row_spec(tm, D_MODEL), _full_spec(w.shape), _full_spec(vg.shape), _full_spec(vb.shape),
                  _full_spec(ws.shape), _full_spec(bs_t.shape)],
        out_specs=[_row_spec(tm, HALF)] * 4,
        out_shape=[half_bf] * 4,
        compiler_params=_params("parallel"),
        name="odd_in_proj_gmlp",
    )(x, w, vg, vb, ws, bs_t)


def _sb_kernel(q_ref, k_ref, v_ref, out_ref, lsz_ref, sp_ref, run_ref, acc_ref):
    nb = q_ref.shape[0]
    tq = q_ref.shape[1]
    tk = SB_KEY_TILE
    ratio = tq // tk
    i = pl.program_id(2)
    lane = lax.broadcasted_iota(jnp.int32, (tq, LANES), 1)
    row = lax.broadcasted_iota(jnp.int32, (tk, tk), 0)
    col = lax.broadcasted_iota(jnp.int32, (tk, tk), 1)
    later = jnp.where(row > col, 1.0, 0.0).astype(BF16)
    qrow = lax.broadcasted_iota(jnp.int32, (tq, tk), 0)
    kcol = lax.broadcasted_iota(jnp.int32, (tq, tk), 1)
    last = ratio * (i + 1) - 1
    streams = []
    for b in range(nb):
        q = q_ref[b]
        zero = jnp.zeros_like(q)
        streams.append((b, 0, jnp.where(lane < SB_HEAD_DIM, q, zero)))
        streams.append((b, 1, jnp.where(lane >= SB_HEAD_DIM, q, zero)))

    def rows_of(k, diagonal):
        return slice((ratio - 1 - k) * tk, tq) if diagonal else slice(None)

    def scores(k, slot, diagonal):
        j = last - k
        rows = rows_of(k, diagonal)
        for b, h, qm in streams:
            kb = k_ref[b, pl.ds(pl.multiple_of(j * tk, tk), tk), :]
            z = _dot_nt(qm[rows], kb)
            sp = jnp.maximum(z, 0.0) + jnp.log(1.0 + jnp.exp2(jnp.abs(z) * -LOG2E))
            lsz = z - sp
            if diagonal:
                visible = (kcol + (ratio - 1 - k) * tk < qrow)[rows]
                sp = jnp.where(visible, sp, 0.0)
                lsz = jnp.where(visible, lsz, MASKED_SCORE)
            sp_ref[slot, b, h, rows] = sp.astype(BF16)
            run = run_ref[b, h, rows]
            for lt in range(tk // LANES):
                lanes = slice(lt * LANES, (lt + 1) * LANES)
                lsz_ref[slot, b, h, rows, lanes] = lsz[:, lanes] - run
            run_ref[b, h, rows] = run + jnp.sum(sp, axis=1, keepdims=True)

    def output(k, slot, diagonal=False):
        j = last - k
        rows = rows_of(k, diagonal)
        for b, h, _ in streams:
            vb = v_ref[b, pl.ds(pl.multiple_of(j * tk, tk), tk), :]
            att = jnp.exp2((lsz_ref[slot, b, h, rows] - _dot(sp_ref[slot, b, h, rows], later)) * LOG2E)
            acc_ref[b, h, rows] += _dot(att.astype(BF16), vb)

    run_ref[...] = jnp.zeros_like(run_ref)
    acc_ref[...] = jnp.zeros_like(acc_ref)
    scores(0, 0, True)
    for k in range(1, ratio):
        output(k - 1, (k - 1) % 2, True)
        scores(k, k % 2, True)

    def two_blocks(p, carry):
        output(2 * p - 1, 1)
        scores(2 * p, 0, False)
        output(2 * p, 0)
        scores(2 * p + 1, 1, False)
        return carry

    lax.fori_loop(ratio // 2, ratio * (i + 1) // 2, two_blocks, 0)
    output(last, 1)
    for b in range(nb):
        out_ref[b] = jnp.where(lane < SB_HEAD_DIM, acc_ref[b, 0], acc_ref[b, 1]).astype(BF16)


def _sb_attention(q, k, v, tq):
    bsz, s, _ = q.shape
    nb = SB_BATCH if bsz % SB_BATCH == 0 else 1
    kv_spec = pl.BlockSpec((nb, s, LANES), lambda bi, p, i: (bi, 0, p))
    q_spec = pl.BlockSpec((nb, tq, LANES), lambda bi, p, i: (bi, i, p))
    heads = LANES // SB_HEAD_DIM
    assert tq % (2 * SB_KEY_TILE) == 0, "the walk pairs key blocks, two scratch slots"
    return pl.pallas_call(
        _sb_kernel,
        grid=(bsz // nb, HALF // LANES, s // tq),
        in_specs=[q_spec, kv_spec, kv_spec],
        out_specs=q_spec,
        out_shape=jax.ShapeDtypeStruct((bsz, s, HALF), BF16),
        scratch_shapes=[pltpu.VMEM((2, nb, heads, tq, SB_KEY_TILE), F32),
                        pltpu.VMEM((2, nb, heads, tq, SB_KEY_TILE), BF16),
                        pltpu.VMEM((nb, heads, tq, LANES), F32), pltpu.VMEM((nb, heads, tq, LANES), F32)],
        compiler_params=_params("parallel", "parallel", "arbitrary"),
        name="stick_breaking_attention",
    )(q, k, v)


def _odd_out_kernel(c_ref, d_ref, x_ref, w_ref, g_ref, beta_ref, rw_ref, rb_ref,
                    x1_ref, gate_ref, route_ref, cnt_ref, base_ref):
    tm = x_ref.shape[0]

    @pl.when(pl.program_id(0) == 0)
    def _():
        base_ref[...] = jnp.zeros_like(base_ref)

    x1 = _mix_ln(c_ref, d_ref, x_ref, w_ref, g_ref, beta_ref)
    x1_ref[...] = x1

    xh, xl = _split_bf16(x1)
    logits = _dot(xh, rw_ref[0]) + _dot(xl, rw_ref[0]) + _dot(xh, rw_ref[1]) + rb_ref[...]
    lane = lax.broadcasted_iota(jnp.int32, (tm, LANES), 1)
    logits = jnp.where(lane < N_EXPERTS, logits, -jnp.inf)
    m1 = jnp.max(logits, axis=1, keepdims=True)
    i1 = jnp.min(jnp.where(logits == m1, lane, LANES), axis=1, keepdims=True)
    sel1 = lane == i1
    rest = jnp.where(sel1, -jnp.inf, logits)
    m2 = jnp.max(rest, axis=1, keepdims=True)
    i2 = jnp.min(jnp.where(rest == m2, lane, LANES), axis=1, keepdims=True)
    sel2 = lane == i2
    e = jnp.exp(m2 - m1)
    g1 = 1.0 / (1.0 + e)
    g2 = e / (1.0 + e)
    first_lower = i1 < i2
    gate_ref[...] = jnp.where(lane == 0, jnp.where(first_lower, g1, g2),
                              jnp.where(lane == 1, jnp.where(first_lower, g2, g1), 0.0))

    sel_f = jnp.where(sel1 | sel2, 1.0, 0.0)
    row = lax.broadcasted_iota(jnp.int32, (tm, tm), 0)
    col = lax.broadcasted_iota(jnp.int32, (tm, tm), 1)
    before = jnp.where(col < row, 1.0, 0.0).astype(BF16)
    rank = base_ref[0:1, :] + _dot(before, sel_f.astype(BF16))
    e_lo = jnp.minimum(i1, i2)
    e_hi = jnp.maximum(i1, i2)
    rank_lo = jnp.sum(jnp.where(lane == e_lo, rank, 0.0), axis=1, keepdims=True)
    rank_hi = jnp.sum(jnp.where(lane == e_hi, rank, 0.0), axis=1, keepdims=True)
    route_ref[...] = jnp.where(lane == 0, e_lo, jnp.where(lane == 1, e_hi, jnp.where(
        lane == 2, rank_lo.astype(jnp.int32), jnp.where(lane == 3, rank_hi.astype(jnp.int32), 0))))
    base_ref[...] = base_ref[...] + jnp.sum(sel_f, axis=0, keepdims=True)
    cnt_ref[...] = base_ref[...].astype(jnp.int32)


def _odd_out(c, d, x, w, g, beta, rw, rb, tm):
    t = x.shape[0]
    lane_spec = _row_spec(tm, LANES)
    return pl.pallas_call(
        _odd_out_kernel,
        grid=(t // tm,),
        in_specs=[_row_spec(tm, HALF), _row_spec(tm, HALF), _row_spec(tm, D_MODEL),
                  _full_spec(w.shape), _full_spec(g.shape), _full_spec(beta.shape),
                  _full_spec(rw.shape), _full_spec(rb.shape)],
        out_specs=[_row_spec(tm, D_MODEL), lane_spec, lane_spec, _full_spec((8, LANES))],
        out_shape=[jax.ShapeDtypeStruct((t, D_MODEL), F32), jax.ShapeDtypeStruct((t, LANES), F32),
                   jax.ShapeDtypeStruct((t, LANES), jnp.int32), jax.ShapeDtypeStruct((8, LANES), jnp.int32)],
        scratch_shapes=[pltpu.VMEM((8, LANES), F32)],
        compiler_params=_params("arbitrary"),
        name="odd_out_proj_ln_router",
    )(c, d, x, w, g, beta, rw, rb)


def _for_rows(n, fn):
    def body(r, carry):
        fn(r)
        return carry
    lax.fori_loop(0, n, body, 0, unroll=8)


def _dispatch_kernel(pos_ref, x_ref, xs_hbm, slab_ref, sem):
    tm = x_ref.shape[0]
    i = pl.program_id(0)
    buf = i % 2
    for s in range(SLAB):
        slab_ref[buf, pl.ds(s, tm, stride=SLAB), :] = x_ref[:, s * LANES:(s + 1) * LANES]

    def start(r):
        src = slab_ref.at[buf, pl.ds(pl.multiple_of(r * SLAB, SLAB), SLAB)]
        for k in range(2):
            dst = pl.multiple_of(pos_ref[0, 0, k * tm + r] * SLAB, SLAB)
            pltpu.make_async_copy(src, xs_hbm.at[pl.ds(dst, SLAB)], sem.at[buf]).start(priority=k)

    _for_rows(tm, start)

    def wait_step(b):
        for _ in range(2):
            pltpu.make_async_copy(slab_ref.at[b], xs_hbm.at[pl.ds(0, tm * SLAB)], sem.at[b]).wait()

    @pl.when(i > 0)
    def _():
        wait_step(1 - buf)

    @pl.when(i == pl.num_programs(0) - 1)
    def _():
        wait_step(buf)


def _dispatch(x1, pos, tm):
    t = x1.shape[0]
    return pl.pallas_call(
        _dispatch_kernel,
        grid=(t // tm,),
        in_specs=[pl.BlockSpec((1, 1, 2 * tm), lambda i: (i, 0, 0), memory_space=pltpu.SMEM),
                  _row_spec(tm, D_MODEL)],
        out_specs=pl.BlockSpec(memory_space=pl.ANY),
        out_shape=jax.ShapeDtypeStruct((2 * t * SLAB, LANES), F32),
        scratch_shapes=[pltpu.VMEM((2, tm * SLAB, LANES), F32), pltpu.SemaphoreType.DMA((2,))],
        compiler_params=_params("arbitrary"),
        name="moe_dispatch",
    )(pos, x1)


def _moe_kernel(tile_ref, expert_ref, lo_ref, hi_ref, xs_ref, w1_ref, w3_ref, w2_ref, y_ref, xb_ref, acc_ref):
    tm = xb_ref.shape[0]
    step = pl.program_id(0)
    lo = lo_ref[step]
    hi = hi_ref[step]

    @pl.when(hi > lo)
    def _():
        for s in range(SLAB):
            xb_ref[:, s * LANES:(s + 1) * LANES] = xs_ref[pl.ds(s, tm, stride=SLAB), :].astype(BF16)
        _swiglu_into(xb_ref, w1_ref, w3_ref, w2_ref, acc_ref)

        @pl.when(lo == 0)
        def _():
            for s in range(SLAB):
                y_ref[pl.ds(s, tm, stride=SLAB), :] = acc_ref[:, s * LANES:(s + 1) * LANES]

        @pl.when(lo > 0)
        def _():
            row = lax.broadcasted_iota(jnp.int32, (tm, LANES), 0)
            for s in range(SLAB):
                rows = pl.ds(s, tm, stride=SLAB)
                y_ref[rows, :] = jnp.where(row >= lo, acc_ref[:, s * LANES:(s + 1) * LANES], y_ref[rows, :])


def _moe(xs, step_tile, step_expert, step_lo, step_hi, w1, w3, w2, layer, tm):
    n_steps = step_tile.shape[0]
    w_spec = lambda shape: pl.BlockSpec((None, None) + shape[2:], lambda s, tile, ex, lo, hi: (layer, ex[s], 0, 0))
    row_spec = pl.BlockSpec((tm * SLAB, LANES), lambda s, tile, ex, lo, hi: (tile[s], 0))
    grid_spec = pltpu.PrefetchScalarGridSpec(
        num_scalar_prefetch=4,
        grid=(n_steps,),
        in_specs=[row_spec, w_spec(w1.shape), w_spec(w3.shape), w_spec(w2.shape)],
        out_specs=row_spec,
        scratch_shapes=[pltpu.VMEM((tm, D_MODEL), BF16), pltpu.VMEM((tm, D_MODEL), F32)],
    )
    return pl.pallas_call(
        _moe_kernel,
        grid_spec=grid_spec,
        out_shape=jax.ShapeDtypeStruct(xs.shape, F32),
        compiler_params=_params("arbitrary"),
        name="moe_grouped_swiglu",
    )(step_tile, step_expert, step_lo, step_hi, xs, w1, w3, w2)


def _combine_kernel(pos_ref, pos_next_ref, x_ref, gate_ref, g_ref, beta_ref, y_hbm, out_ref, ybuf, ffn_ref, sem):
    tm = x_ref.shape[0]
    i = pl.program_id(0)
    slot = i % 2

    def start_tile(p_ref, buf):
        def start(r):
            for k in range(2):
                src = pl.multiple_of(p_ref[0, 0, k * tm + r] * SLAB, SLAB)
                pltpu.make_async_copy(y_hbm.at[pl.ds(src, SLAB)],
                                      ybuf.at[buf, k, pl.ds(pl.multiple_of(r * SLAB, SLAB), SLAB)],
                                      sem.at[buf]).start(priority=k)
        _for_rows(tm, start)

    @pl.when(i == 0)
    def _():
        start_tile(pos_ref, 0)

    @pl.when(i + 1 < pl.num_programs(0))
    def _():
        start_tile(pos_next_ref, 1 - slot)

    for k in range(2):
        pltpu.make_async_copy(y_hbm.at[pl.ds(0, tm * SLAB)], ybuf.at[slot, k], sem.at[slot]).wait()
    gate = gate_ref[...]
    for s in range(SLAB):
        rows = pl.ds(s, tm, stride=SLAB)
        ffn_ref[:, s * LANES:(s + 1) * LANES] = (gate[:, 0:1] * ybuf[slot, 0, rows, :]
                                                 + gate[:, 1:2] * ybuf[slot, 1, rows, :])
    out_ref[...] = _ln(ALPHA * x_ref[...] + ffn_ref[...], g_ref[...], beta_ref[...])


def _combine(x1, y, pos, gate, g, beta, tm):
    t = x1.shape[0]
    nt = t // tm
    pos_spec = lambda f: pl.BlockSpec((1, 1, 2 * tm), f, memory_space=pltpu.SMEM)
    return pl.pallas_call(
        _combine_kernel,
        grid=(nt,),
        in_specs=[pos_spec(lambda i: (i, 0, 0)), pos_spec(lambda i: (jnp.minimum(i + 1, nt - 1), 0, 0)),
                  _row_spec(tm, D_MODEL), _row_spec(tm, LANES), _full_spec(g.shape), _full_spec(beta.shape),
                  pl.BlockSpec(memory_space=pl.ANY)],
        out_specs=_row_spec(tm, D_MODEL),
        out_shape=jax.ShapeDtypeStruct((t, D_MODEL), F32),
        scratch_shapes=[pltpu.VMEM((2, 2, tm * SLAB, LANES), F32), pltpu.VMEM((tm, D_MODEL), F32),
                        pltpu.SemaphoreType.DMA((2,))],
        compiler_params=_params("arbitrary"),
        name="moe_combine_ln",
    )(pos, pos, x1, gate, g, beta, y)


def _routing_tables(route, cnt, t, tm):
    counts = cnt[0, :N_EXPERTS]
    ends = jnp.cumsum(counts)
    starts = ends - counts
    experts = jnp.arange(N_EXPERTS, dtype=jnp.int32)

    def sorted_row(e, rank):
        return jnp.sum(jnp.where(e[:, None] == experts[None, :], starts[None, :], 0), axis=1) + rank

    pos = jnp.stack([sorted_row(route[:, 0], route[:, 2]), sorted_row(route[:, 1], route[:, 3])])
    nt = t // tm
    pos = pos.reshape(2, nt, tm).transpose(1, 0, 2).reshape(nt, 1, 2 * tm)

    n_steps = (2 * t) // tm + N_EXPERTS - 1
    first_tile = starts // tm
    n_visits = jnp.where(counts > 0, (ends - 1) // tm - first_tile + 1, 0)
    step_ends = jnp.cumsum(n_visits)
    total = step_ends[-1]
    s = jnp.minimum(jnp.arange(n_steps, dtype=jnp.int32), total - 1)
    step_expert = jnp.sum((s[:, None] >= step_ends[None, :]).astype(jnp.int32), axis=1)
    step_tile = first_tile[step_expert] + s - (step_ends - n_visits)[step_expert]
    lo = jnp.clip(starts[step_expert] - step_tile * tm, 0, tm)
    hi = jnp.clip(ends[step_expert] - step_tile * tm, 0, tm)
    hi = jnp.where(jnp.arange(n_steps) < total, hi, lo)
    as_i32 = lambda a: a.astype(jnp.int32)
    return pos.astype(jnp.int32), as_i32(step_tile), as_i32(step_expert), as_i32(lo), as_i32(hi)


def _row(v):
    return v.reshape(1, -1).astype(F32)


def _even_layer(x, bsz, s, w_in, gate_bias, conv_w, conv_b, norm_g, norm_b, head_g, w_out,
                ln1_g, ln1_b, w1, w3, w2, layer, ln2_g, ln2_b):
    main = 6 * HALF
    w_main = w_in[:, :main].astype(BF16)
    w_gate = jnp.pad(w_in[:, main:], ((0, 0), (0, LANES - 2 * MLSTM_HEADS))).astype(BF16)
    gb = jnp.pad(gate_bias, (0, LANES - 2 * MLSTM_HEADS)).reshape(1, LANES)
    glu, q, k, v, o, gates, kt, gates_t = _even_in(x, w_main, w_gate, gb, min(PROJ_TILE, x.shape[0]))
    shp = lambda a: a.reshape(bsz, s, a.shape[-1])
    a = _conv_branch(shp(glu), conv_w, conv_b, _row(norm_g), _row(norm_b), min(s, 512))
    hb = _mlstm_branch(shp(q), shp(k), kt, shp(v), shp(o), shp(gates), gates_t, _row(head_g))
    return _mix_ffn(a.reshape(-1, HALF), hb.reshape(-1, HALF), x, w_out.astype(BF16), _row(ln1_g), _row(ln1_b),
                    w1, w3, w2, layer, _row(ln2_g), _row(ln2_b), TOKEN_TILE)


def _odd_layer(x, bsz, s, w_in, v_g, v_b, w_s, b_s, w_out, ln1_g, ln1_b, router_w, router_b,
               w1, w3, w2, layer, ln2_g, ln2_b):
    t = x.shape[0]
    c, q, k, v = _odd_in(x, w_in.astype(BF16), _row(v_g), _row(v_b), w_s, b_s.T, min(PROJ_TILE, t))
    shp = lambda a: a.reshape(bsz, s, HALF)
    d = _sb_attention(shp(q), shp(k), shp(v), min(s, SB_TILE)).reshape(t, HALF)
    rw = jnp.pad(router_w, ((0, 0), (0, LANES - N_EXPERTS)))
    rw_hi = rw.astype(BF16)
    rw_lo = (rw - rw_hi.astype(F32)).astype(BF16)
    rb = jnp.pad(router_b, (0, LANES - N_EXPERTS)).reshape(1, LANES)
    x1, gate, route, cnt = _odd_out(c, d, x, w_out.astype(BF16), _row(ln1_g), _row(ln1_b),
                                    jnp.stack([rw_hi, rw_lo]), rb, min(PROJ_TILE, t))
    pos, step_tile, step_expert, step_lo, step_hi = _routing_tables(route, cnt, t, MOE_TILE)
    xs = _dispatch(x1, pos, MOE_TILE)
    y = _moe(xs, step_tile, step_expert, step_lo, step_hi, w1, w3, w2, layer, MOE_TILE)
    return _combine(x1, y, pos, gate, _row(ln2_g), _row(ln2_b), MOE_TILE)


def kernel(x, ab_w_in, ab_gate_bias, a_conv_w, a_conv_b, a_norm_g, a_norm_b, b_norm_g, ab_w_out, ab_ln1_g, ab_ln1_b, ffn_w1, ffn_w3, ffn_w2, ab_ln2_g, ab_ln2_b, cd_w_in, c_norm_g, c_norm_b, c_w_s, c_b_s, cd_w_out, cd_ln1_g, cd_ln1_b, router_w, router_b, moe_w1, moe_w3, moe_w2, cd_ln2_g, cd_ln2_b):
    bsz, s, _ = x.shape
    h = x.reshape(bsz * s, D_MODEL)
    ffn_w = [w.astype(BF16) for w in (ffn_w1, ffn_w3, ffn_w2)]
    moe_w = [w.astype(BF16) for w in (moe_w1, moe_w3, moe_w2)]
    for layer in range(DEPTH):
        j = layer // 2
        if layer % 2 == 0:
            h = _even_layer(h, bsz, s, ab_w_in[j], ab_gate_bias[j], a_conv_w[j], a_conv_b[j], a_norm_g[j],
                            a_norm_b[j], b_norm_g[j], ab_w_out[j], ab_ln1_g[j], ab_ln1_b[j],
                            *ffn_w, j, ab_ln2_g[j], ab_ln2_b[j])
        else:
            h = _odd_layer(h, bsz, s, cd_w_in[j], c_norm_g[j], c_norm_b[j], c_w_s[j], c_b_s[j], cd_w_out[j],
                           cd_ln1_g[j], cd_ln1_b[j], router_w[j], router_b[j],
                           *moe_w, j, cd_ln2_g[j], cd_ln2_b[j])
    return h.reshape(bsz, s, D_MODEL)
```
